```python
import jax, jax.numpy as jnp
from jax import lax
import numpy as np

D_MODEL = 2048
BATCH = 4
SEQ = 2048
DEPTH = 4

HEAD_DIM = 128
N_HEADS = D_MODEL // HEAD_DIM
BRANCH_WIDTH = N_HEADS * HEAD_DIM
DSWA_PATTERNS = ((128, 1), (512, 4), (2048, 16))
N_GROUPS = 3
DSWA_QBLOCK = 128
MOBA_BLOCK = 256
MOBA_TOPK = 3
MOBA_QCHUNK = 16
N_MIXERS = 2
N_A = (DEPTH + 1) // 2
N_B = DEPTH // 2
A_IN_WIDTH = (3 * N_GROUPS + 1) * BRANCH_WIDTH
B_IN_WIDTH = 4 * BRANCH_WIDTH
EPS = 1e-6
NEG = -1e30

kernel_name = "hybrid_dilated_moba_adaln_trunk"


def rms_norm(x, w):
    xf = x.astype(jnp.float32)
    y = xf * lax.rsqrt(jnp.mean(xf * xf, axis=-1, keepdims=True) + EPS) * w.astype(jnp.float32)
    return y.astype(x.dtype)


def alibi_slopes(n):
    return jnp.exp2(-8.0 * jnp.arange(1, n + 1, dtype=jnp.float32) / n)


def dilated_window_attention(q, k, v, window, dilation, slopes):
    B, S, H, Dh = q.shape
    steps = window // dilation
    QB = DSWA_QBLOCK
    L = S // dilation
    nq = -(-L // QB)
    Lp = nq * QB

    def to_residue(a):
        a = a.reshape(B, L, dilation, H, Dh).transpose(0, 2, 1, 3, 4)
        return jnp.pad(a, ((0, 0), (0, 0), (0, Lp - L), (0, 0), (0, 0)))

    def band(a):
        cur = a.reshape(B, dilation, nq, QB, H, Dh)
        prev = jnp.pad(a, ((0, 0), (0, 0), (QB, 0), (0, 0), (0, 0)))[:, :, :Lp]
        prev = prev.reshape(B, dilation, nq, QB, H, Dh)
        return jnp.concatenate([prev, cur], axis=3)

    qr, kr, vr = to_residue(q), to_residue(k), to_residue(v)
    qb = qr.reshape(B, dilation, nq, QB, H, Dh)
    kw, vw = band(kr), band(vr)
    s = jnp.einsum('brnqhd,brnkhd->brnhqk', qb, kw).astype(jnp.float32) * (Dh ** -0.5)
    qi = jnp.arange(QB)[:, None]
    ki = jnp.arange(2 * QB)[None, :]
    step = qi + QB - ki
    key_l = jnp.arange(nq)[:, None, None] * QB + ki[None] - QB
    ok = ((step >= 0) & (step <= steps))[None] & (key_l >= 0)
    bias = -slopes[:, None, None] * (step * dilation).astype(jnp.float32)
    s = jnp.where(ok[:, None], s + bias, NEG)
    lse = jax.nn.logsumexp(s, axis=-1)
    p = jnp.exp(s - lse[..., None]).astype(v.dtype)
    o = jnp.einsum('brnhqk,brnkhd->brnqhd', p, vw)

    def from_residue(a):
        tail = a.shape[4:]
        a = a.reshape((B, dilation, Lp) + tail)[:, :, :L]
        a = jnp.moveaxis(a, 1, 2)
        return a.reshape((B, S) + tail)

    return from_residue(o), from_residue(jnp.swapaxes(lse, 3, 4))


def dilated_mixer(h, w_in, slopes):
    B, S, _ = h.shape
    parts = jnp.split(h @ w_in, 3 * N_GROUPS + 1, axis=-1)
    heads = lambda a: a.reshape(B, S, N_HEADS, HEAD_DIM)
    outs, lses = [], []
    for g, (window, dilation) in enumerate(DSWA_PATTERNS):
        o, lse = dilated_window_attention(heads(parts[3 * g]), heads(parts[3 * g + 1]),
                                          heads(parts[3 * g + 2]), window, dilation, slopes)
        outs.append(o)
        lses.append(lse)
    alpha = jax.nn.softmax(jnp.stack(lses), axis=0)
    o = jnp.einsum('gbsh,gbshd->bshd', alpha, jnp.stack(outs).astype(jnp.float32))
    return o.reshape(B, S, BRANCH_WIDTH).astype(h.dtype) * jax.nn.silu(parts[-1])


def moba_attention(q, k, v, slopes):
    B, S, H, Dh = q.shape
    BLK = MOBA_BLOCK
    nblk = -(-S // BLK)
    Sp = nblk * BLK
    q, k, v = [jnp.pad(a, ((0, 0), (0, Sp - S), (0, 0), (0, 0))).transpose(0, 2, 1, 3)
               for a in (q, k, v)]
    scale = Dh ** -0.5
    qb = q.reshape(B, H, nblk, BLK, Dh)
    kb = k.reshape(B, H, nblk, BLK, Dh)
    vb = v.reshape(B, H, nblk, BLK, Dh)

    s = jnp.einsum('bhnqd,bhnkd->bhnqk', qb, kb).astype(jnp.float32) * scale
    dist = jnp.arange(BLK)[:, None] - jnp.arange(BLK)[None, :]
    s = jnp.where(dist >= 0, s - slopes[:, None, None, None] * dist.astype(jnp.float32), NEG)
    lse_own = jax.nn.logsumexp(s, axis=-1)
    o_own = jnp.einsum('bhnqk,bhnkd->bhnqd', jnp.exp(s - lse_own[..., None]).astype(v.dtype), vb)
    lse_own = lse_own.reshape(B, H, Sp)
    o_own = o_own.reshape(B, H, Sp, Dh).astype(jnp.float32)

    n_sel = min(MOBA_TOPK, nblk - 1)
    if n_sel == 0:
        o = o_own
    else:
        pos = jnp.arange(Sp)
        qblk = pos // BLK
        kmean = kb.astype(jnp.float32).mean(axis=3).astype(kb.dtype)
        gate = jnp.einsum('bhtd,bhnd->bhtn', q, kmean).astype(jnp.float32)
        past = jnp.arange(nblk)[None, :] < qblk[:, None]
        gate = jnp.where(past, gate, NEG)
        _, idx = lax.top_k(gate, n_sel)
        valid = jnp.arange(n_sel)[None, :] < qblk[:, None]

        QC = MOBA_QCHUNK
        nC = Sp // QC
        q_c = jnp.moveaxis(q.reshape(B, H, nC, QC, Dh), 2, 0)
        i_c = jnp.moveaxis(idx.reshape(B, H, nC, QC, n_sel), 2, 0)
        t_c = pos.reshape(nC, QC)
        v_c = valid.reshape(nC, QC, n_sel)
        bi = jnp.arange(B)[:, None, None, None]
        hi = jnp.arange(H)[None, :, None, None]

        def attend(args):
            qc, ic, tc, vc = args
            kg = kb[bi, hi, ic]
            vg = vb[bi, hi, ic]
            sc = jnp.einsum('bhqd,bhqnkd->bhqnk', qc, kg).astype(jnp.float32) * scale
            kpos = ic[..., None] * BLK + jnp.arange(BLK)
            dd = (tc[:, None, None] - kpos).astype(jnp.float32)
            sc = jnp.where(vc[:, :, None], sc - slopes[:, None, None, None] * dd, NEG)
            sc = sc.reshape(B, H, QC, n_sel * BLK)
            lse = jax.nn.logsumexp(sc, axis=-1)
            p = jnp.exp(sc - lse[..., None]).astype(v.dtype).reshape(B, H, QC, n_sel, BLK)
            return jnp.einsum('bhqnk,bhqnkd->bhqd', p, vg), lse

        o_sel, lse_sel = lax.map(attend, (q_c, i_c, t_c, v_c))
        o_sel = jnp.moveaxis(o_sel, 0, 2).reshape(B, H, Sp, Dh).astype(jnp.float32)
        lse_sel = jnp.moveaxis(lse_sel, 0, 2).reshape(B, H, Sp)
        m = jnp.logaddexp(lse_own, lse_sel)
        o = (jnp.exp(lse_own - m)[..., None] * o_own
             + jnp.exp(lse_sel - m)[..., None] * o_sel)
    return o.transpose(0, 2, 1, 3)[:, :S]


def moba_mixer(h, w_in, slopes):
    B, S, _ = h.shape
    q, k, v, z = jnp.split(h @ w_in, 4, axis=-1)
    heads = lambda a: a.reshape(B, S, N_HEADS, HEAD_DIM)
    o = moba_attention(heads(q), heads(k), heads(v), slopes)
    return o.reshape(B, S, BRANCH_WIDTH).astype(h.dtype) * jax.nn.silu(z)


def setup_inputs(seed: int = 0) -> dict:
    key = jax.random.key(seed)
    ks = jax.random.split(key, 10)
    f32 = jnp.float32
    x = jax.random.normal(ks[0], (BATCH, SEQ, D_MODEL), f32)
    c = jax.random.normal(ks[1], (BATCH, D_MODEL), f32)
    norm_w = 1.0 + 0.02 * jax.random.normal(ks[2], (DEPTH, D_MODEL), f32)
    mod_w = jax.random.normal(ks[3], (DEPTH, D_MODEL, 3 * D_MODEL), f32) * (0.5 * D_MODEL ** -0.5)
    mod_b = 0.02 * jax.random.normal(ks[4], (DEPTH, 3 * D_MODEL), f32)
    a_w_in = jax.random.normal(ks[5], (N_A, D_MODEL, A_IN_WIDTH), f32) * D_MODEL ** -0.5
    a_w_out = jax.random.normal(ks[6], (N_A, BRANCH_WIDTH, D_MODEL), f32) * BRANCH_WIDTH ** -0.5
    b_w_in = jax.random.normal(ks[7], (N_B, D_MODEL, B_IN_WIDTH), f32) * D_MODEL ** -0.5
    b_w_out = jax.random.normal(ks[8], (N_B, BRANCH_WIDTH, D_MODEL), f32) * BRANCH_WIDTH ** -0.5
    final_norm_w = 1.0 + 0.02 * jax.random.normal(ks[9], (D_MODEL,), f32)
    return {"x": x, "c": c, "norm_w": norm_w, "mod_w": mod_w, "mod_b": mod_b,
            "a_w_in": a_w_in, "a_w_out": a_w_out, "b_w_in": b_w_in, "b_w_out": b_w_out,
            "final_norm_w": final_norm_w}


def reference(x, c, norm_w, mod_w, mod_b, a_w_in, a_w_out, b_w_in, b_w_out, final_norm_w):
    slopes = alibi_slopes(N_HEADS)
    cond = jax.nn.silu(c)
    for i in range(DEPTH):
        mod = cond @ mod_w[i] + mod_b[i]
        shift, scale, gate = jnp.split(mod[:, None, :], 3, axis=-1)
        h = rms_norm(x, norm_w[i]) * (1.0 + scale) + shift
        j = i // N_MIXERS
        if i % N_MIXERS == 0:
            y = dilated_mixer(h, a_w_in[j], slopes) @ a_w_out[j]
        else:
            y = moba_mixer(h, b_w_in[j], slopes) @ b_w_out[j]
        x = x + gate * y
    return rms_norm(x, final_norm_w)
```

```python
import functools

import jax
import jax.numpy as jnp
from jax import lax
from jax.experimental import pallas as pl
from jax.experimental.pallas import tpu as pltpu

F32 = jnp.float32
BF16 = jnp.bfloat16

HEAD_DIM = 128
N_HEADS = 16
DSWA_PATTERNS = ((128, 1), (512, 4), (2048, 16))
N_GROUPS = len(DSWA_PATTERNS)
QB = 128
MOBA_BLOCK = 256
MOBA_TOPK = 3
EPS = 1e-6
NEG = -1e30
SM_SCALE = HEAD_DIM ** -0.5

VMEM_LIMIT_BYTES = 56 * 1024 * 1024
MOD_PAD_ROWS = 16


def _params(*sem):
    return pltpu.CompilerParams(dimension_semantics=sem, vmem_limit_bytes=VMEM_LIMIT_BYTES)


def _mod_kernel(c_ref, w_ref, b_ref, o_ref):
    c = c_ref[...]
    cond = c / (1.0 + jnp.exp(-c))
    acc = jnp.dot(cond.astype(BF16), w_ref[...].astype(BF16), preferred_element_type=F32)
    o_ref[...] = acc + b_ref[...]


def _modulation(c, mod_w, mod_b):
    depth, d, n = mod_w.shape
    b = c.shape[0]
    tn = n // 4
    c_pad = jnp.pad(c, ((0, MOD_PAD_ROWS - b), (0, 0)))
    out = pl.pallas_call(
        _mod_kernel,
        grid=(depth, n // tn),
        in_specs=[
            pl.BlockSpec((MOD_PAD_ROWS, d), lambda l, j: (0, 0)),
            pl.BlockSpec((None, d, tn), lambda l, j: (l, 0, j)),
            pl.BlockSpec((None, 1, tn), lambda l, j: (l, 0, j)),
        ],
        out_specs=pl.BlockSpec((None, MOD_PAD_ROWS, tn), lambda l, j: (l, 0, j)),
        out_shape=jax.ShapeDtypeStruct((depth, MOD_PAD_ROWS, n), F32),
        compiler_params=_params("parallel", "arbitrary"),
        name="adaln_modulation",
    )(c_pad, mod_w, mod_b.reshape(depth, 1, n))
    return out[:, :b].reshape(depth, b, 3, d)


def _inproj_kernel(x_ref, nw_ref, mod_ref, w_ref, o_ref, h_ref):
    @pl.when(pl.program_id(1) == 0)
    def _():
        x = x_ref[...]
        ms = jnp.mean(x * x, axis=-1, keepdims=True)
        y = x * lax.rsqrt(ms + EPS) * nw_ref[...]
        h = y * (1.0 + mod_ref[1:2, :]) + mod_ref[0:1, :]
        h_ref[...] = h.astype(BF16)

    acc = jnp.dot(h_ref[...], w_ref[...].astype(BF16), preferred_element_type=F32)
    o_ref[...] = acc.astype(o_ref.dtype)


def _in_proj(xf, norm_w3, mod, w_in, layer, widx, seq):
    m, d = xf.shape
    n = w_in.shape[-1]
    tm, tn = 1024, 1024
    tiles_per_batch = seq // tm
    return pl.pallas_call(
        _inproj_kernel,
        grid=(m // tm, n // tn),
        in_specs=[
            pl.BlockSpec((tm, d), lambda i, j: (i, 0)),
            pl.BlockSpec((None, 1, d), lambda i, j: (layer, 0, 0)),
            pl.BlockSpec((None, None, 3, d), lambda i, j: (layer, i // tiles_per_batch, 0, 0)),
            pl.BlockSpec((None, d, tn), lambda i, j: (widx, 0, j)),
        ],
        out_specs=pl.BlockSpec((tm, tn), lambda i, j: (i, j)),
        out_shape=jax.ShapeDtypeStruct((m, n), BF16),
        scratch_shapes=[pltpu.VMEM((tm, d), BF16)],
        compiler_params=_params("parallel", "arbitrary"),
        name=f"in_proj_l{layer}",
    )(xf, norm_w3, mod, w_in)


def _outproj_kernel(og_ref, w_ref, x_ref, mod_ref, o_ref):
    y = jnp.dot(og_ref[...], w_ref[...].astype(BF16), preferred_element_type=F32)
    o_ref[...] = x_ref[...] + mod_ref[2:3, :] * y


def _out_proj(og, w_out, xf, mod, layer, widx, seq):
    m, k = og.shape
    n = w_out.shape[-1]
    tm, tn = 1024, 1024
    tiles_per_batch = seq // tm
    return pl.pallas_call(
        _outproj_kernel,
        grid=(m // tm, n // tn),
        in_specs=[
            pl.BlockSpec((tm, k), lambda i, j: (i, 0)),
            pl.BlockSpec((None, k, tn), lambda i, j: (widx, 0, j)),
            pl.BlockSpec((tm, tn), lambda i, j: (i, j)),
            pl.BlockSpec((None, None, 3, tn), lambda i, j: (layer, i // tiles_per_batch, 0, j)),
        ],
        out_specs=pl.BlockSpec((tm, tn), lambda i, j: (i, j)),
        out_shape=jax.ShapeDtypeStruct((m, n), F32),
        compiler_params=_params("parallel", "arbitrary"),
        name=f"out_proj_l{layer}",
    )(og, w_out, xf, mod)


def _final_norm_kernel(x_ref, w_ref, o_ref):
    x = x_ref[...]
    ms = jnp.mean(x * x, axis=-1, keepdims=True)
    o_ref[...] = x * lax.rsqrt(ms + EPS) * w_ref[...]


def _final_norm(xf, w):
    m, d = xf.shape
    tm = 512
    return pl.pallas_call(
        _final_norm_kernel,
        grid=(m // tm,),
        in_specs=[pl.BlockSpec((tm, d), lambda i: (i, 0)), pl.BlockSpec((1, d), lambda i: (0, 0))],
        out_specs=pl.BlockSpec((tm, d), lambda i: (i, 0)),
        out_shape=jax.ShapeDtypeStruct((m, d), F32),
        compiler_params=_params("parallel"),
        name="final_norm",
    )(xf, w.reshape(1, d))


def _nt_dot(a, b):
    return lax.dot_general(a, b, (((1,), (1,)), ((), ())), preferred_element_type=F32)


def _band_bias(slope, dilation, with_prev):
    nk = 2 * QB if with_prev else QB
    qi = lax.broadcasted_iota(jnp.int32, (QB, nk), 0)
    ki = lax.broadcasted_iota(jnp.int32, (QB, nk), 1)
    step = qi - ki + (QB if with_prev else 0)
    ok = (step >= 0) & (step <= QB)
    return jnp.where(ok, (-slope * dilation) * step.astype(F32), NEG)


def _softmax_block(q, k, v, bias):
    s = _nt_dot(q, k) * SM_SCALE + bias
    m = jnp.max(s, axis=-1, keepdims=True)
    p = jnp.exp(s - m)
    l = jnp.sum(p, axis=-1, keepdims=True)
    acc = jnp.dot(p.astype(BF16), v, preferred_element_type=F32)
    return acc * (1.0 / l), m + jnp.log(l)


def _dilated_kernel(slopes_ref, q0, k0, v0, q1, k1, v1, q2, k2, v2, z_ref, o_ref,
                    sq1, sk1, sv1, sq2, sk2, sv2, so1, sl1, so2, sl2):
    slope = slopes_ref[pl.program_id(1)]

    for src, dst in ((q1, sq1), (k1, sk1), (v1, sv1), (q2, sq2), (k2, sk2), (v2, sv2)):
        dst[...] = src[...].astype(F32)

    def strided(ref, start, size, stride):
        return ref[pl.ds(start, size, stride=stride), :].astype(BF16)

    def store_block(o_scr, l_scr, start, stride, o, lse):
        o_scr[pl.ds(start, QB, stride=stride), :] = o
        l_scr[pl.ds(start, QB, stride=stride), :] = jnp.broadcast_to(lse, (QB, HEAD_DIM))

    d1 = DSWA_PATTERNS[1][1]
    nq1 = q1.shape[0] // d1 // QB
    bias1_c = _band_bias(slope, d1, False)
    bias1_pc = _band_bias(slope, d1, True)

    def group1(r, carry):
        o, lse = _softmax_block(strided(sq1, r, QB, d1), strided(sk1, r, QB, d1),
                                strided(sv1, r, QB, d1), bias1_c)
        store_block(so1, sl1, r, d1, o, lse)
        for n in range(1, nq1):
            qs = r + n * QB * d1
            ks = r + (n - 1) * QB * d1
            o, lse = _softmax_block(strided(sq1, qs, QB, d1), strided(sk1, ks, 2 * QB, d1),
                                    strided(sv1, ks, 2 * QB, d1), bias1_pc)
            store_block(so1, sl1, qs, d1, o, lse)
        return carry

    lax.fori_loop(0, d1, group1, 0)

    d2 = DSWA_PATTERNS[2][1]
    bias2_c = _band_bias(slope, d2, False)

    def group2(r, carry):
        o, lse = _softmax_block(strided(sq2, r, QB, d2), strided(sk2, r, QB, d2),
                                strided(sv2, r, QB, d2), bias2_c)
        store_block(so2, sl2, r, d2, o, lse)
        return carry

    lax.fori_loop(0, d2, group2, 0)

    d0 = DSWA_PATTERNS[0][1]
    bias0_c = _band_bias(slope, d0, False)
    bias0_pc = _band_bias(slope, d0, True)

    def finish(start, o0, lse0):
        rows = pl.ds(start, QB)
        l0 = jnp.broadcast_to(lse0, (QB, HEAD_DIM))
        l1 = sl1[rows, :]
        l2 = sl2[rows, :]
        mx = jnp.maximum(jnp.maximum(l0, l1), l2)
        w0 = jnp.exp(l0 - mx)
        w1 = jnp.exp(l1 - mx)
        w2 = jnp.exp(l2 - mx)
        o = (w0 * o0 + w1 * so1[rows, :] + w2 * so2[rows, :]) / (w0 + w1 + w2)
        z = z_ref[rows, :].astype(F32)
        o_ref[rows, :] = (o * (z / (1.0 + jnp.exp(-z)))).astype(o_ref.dtype)

    o0, lse0 = _softmax_block(q0[pl.ds(0, QB), :], k0[pl.ds(0, QB), :], v0[pl.ds(0, QB), :], bias0_c)
    finish(0, o0, lse0)

    def group0(n, carry):
        qs = pl.multiple_of(n * QB, QB)
        ks = pl.multiple_of((n - 1) * QB, QB)
        o0, lse0 = _softmax_block(q0[pl.ds(qs, QB), :], k0[pl.ds(ks, 2 * QB), :],
                                  v0[pl.ds(ks, 2 * QB), :], bias0_pc)
        finish(qs, o0, lse0)
        return carry

    lax.fori_loop(1, q0.shape[0] // QB, group0, 0)


def _dilated_attention(proj, slopes):
    b, s, _ = proj.shape
    blk = (None, s, HEAD_DIM)

    def col(c):
        return pl.BlockSpec(blk, lambda bi, hi, sl: (bi, 0, c * N_HEADS + hi))

    grid_spec = pltpu.PrefetchScalarGridSpec(
        num_scalar_prefetch=1,
        grid=(b, N_HEADS),
        in_specs=[col(c) for c in range(3 * N_GROUPS + 1)],
        out_specs=pl.BlockSpec(blk, lambda bi, hi, sl: (bi, 0, hi)),
        scratch_shapes=[pltpu.VMEM((s, HEAD_DIM), F32) for _ in range(10)],
    )
    return pl.pallas_call(
        _dilated_kernel,
        grid_spec=grid_spec,
        out_shape=jax.ShapeDtypeStruct((b, s, N_HEADS * HEAD_DIM), BF16),
        compiler_params=_params("parallel", "arbitrary"),
        name="dilated_mixer",
    )(slopes, *([proj] * (3 * N_GROUPS + 1)))


def _moba_kernel(slopes_ref, q_ref, k_ref, v_ref, z_ref, o_ref, qa_ref, ka_ref):
    slope = slopes_ref[pl.program_id(1)]
    s_len = q_ref.shape[0]
    nblk = s_len // MOBA_BLOCK
    blk = MOBA_BLOCK

    row = lax.broadcasted_iota(jnp.int32, (HEAD_DIM, s_len), 0)
    pos = lax.broadcasted_iota(jnp.int32, (HEAD_DIM, s_len), 1)
    ind = jnp.where(pos // blk == row, 1.0 / blk, 0.0).astype(BF16)
    kmean = jnp.dot(ind, k_ref[...], preferred_element_type=F32)
    k_hi = kmean.astype(BF16)
    k_lo = (kmean - k_hi.astype(F32)).astype(BF16)
    q = q_ref[...]
    gate = _nt_dot(q, k_hi) + _nt_dot(q, k_lo)

    qblk = lax.broadcasted_iota(jnp.int32, (s_len, HEAD_DIM), 0) // blk
    lane = lax.broadcasted_iota(jnp.int32, (s_len, HEAD_DIM), 1)
    rank = jnp.zeros((s_len, HEAD_DIM), jnp.int32)
    for mp in range(nblk - 1):
        g_mp = gate[:, mp:mp + 1]
        beats = (g_mp > gate) | ((g_mp == gate) & (mp < lane))
        rank = rank + jnp.where(beats & (mp < qblk), 1, 0)
    selected = (lane < qblk) & (rank < MOBA_TOPK)

    qa_ref[:, :HEAD_DIM] = q
    qa_ref[:, HEAD_DIM:] = jnp.where(selected, 0.0, NEG).astype(BF16)
    ka_ref[:, :HEAD_DIM] = k_ref[...]
    ka_ref[:, HEAD_DIM:] = jnp.where(lane == qblk, 1.0, 0.0).astype(BF16)

    tq = lax.broadcasted_iota(jnp.int32, (blk, blk), 0)
    sk = lax.broadcasted_iota(jnp.int32, (blk, blk), 1)
    key_bias = slope * lax.broadcasted_iota(jnp.int32, (1, blk), 1).astype(F32)
    own_bias = jnp.where(tq >= sk, slope * sk.astype(F32), NEG)

    def qblock(n, carry):
        qs = pl.multiple_of(n * blk, blk)
        rows = pl.ds(qs, blk)
        s = _nt_dot(q_ref[rows, :], k_ref[rows, :]) * SM_SCALE + own_bias
        m0 = jnp.max(s, axis=-1, keepdims=True)
        p = jnp.exp(s - m0)
        l0 = jnp.sum(p, axis=-1, keepdims=True)
        acc0 = jnp.dot(p.astype(BF16), v_ref[rows, :], preferred_element_type=F32)
        qa = qa_ref[rows, :]

        def past(mb, st):
            m_i, l_i, acc = st
            ks = pl.multiple_of(mb * blk, blk)
            krows = pl.ds(ks, blk)
            bias = key_bias - (slope * blk) * (n - mb).astype(F32)
            s = _nt_dot(qa, ka_ref[krows, :]) * SM_SCALE + bias
            m_new = jnp.maximum(m_i, jnp.max(s, axis=-1, keepdims=True))
            alpha = jnp.exp(m_i - m_new)
            p = jnp.exp(s - m_new)
            l_new = alpha * l_i + jnp.sum(p, axis=-1, keepdims=True)
            acc_new = alpha * acc + jnp.dot(p.astype(BF16), v_ref[krows, :], preferred_element_type=F32)
            return m_new, l_new, acc_new

        _, l_f, acc_f = lax.fori_loop(0, n, past, (m0, l0, acc0))
        z = z_ref[rows, :].astype(F32)
        o_ref[rows, :] = (acc_f * (1.0 / l_f) * (z / (1.0 + jnp.exp(-z)))).astype(o_ref.dtype)
        return carry

    lax.fori_loop(0, nblk, qblock, 0)


def _moba_attention(proj, slopes):
    b, s, _ = proj.shape
    blk = (None, s, HEAD_DIM)

    def col(c):
        return pl.BlockSpec(blk, lambda bi, hi, sl: (bi, 0, c * N_HEADS + hi))

    grid_spec = pltpu.PrefetchScalarGridSpec(
        num_scalar_prefetch=1,
        grid=(b, N_HEADS),
        in_specs=[col(c) for c in range(4)],
        out_specs=pl.BlockSpec(blk, lambda bi, hi, sl: (bi, 0, hi)),
        scratch_shapes=[pltpu.VMEM((s, 2 * HEAD_DIM), BF16), pltpu.VMEM((s, 2 * HEAD_DIM), BF16)],
    )
    return pl.pallas_call(
        _moba_kernel,
        grid_spec=grid_spec,
        out_shape=jax.ShapeDtypeStruct((b, s, N_HEADS * HEAD_DIM), BF16),
        compiler_params=_params("parallel", "arbitrary"),
        name="moba_mixer",
    )(slopes, proj, proj, proj, proj)


def kernel(x, c, norm_w, mod_w, mod_b, a_w_in, a_w_out, b_w_in, b_w_out, final_norm_w):
    b, s, d = x.shape
    depth = norm_w.shape[0]
    slopes = jnp.exp2(-8.0 * jnp.arange(1, N_HEADS + 1, dtype=F32) / N_HEADS)
    mod = _modulation(c, mod_w, mod_b)
    norm_w3 = norm_w.reshape(depth, 1, d)
    xf = x.reshape(b * s, d)
    for i in range(depth):
        j = i // 2
        if i % 2 == 0:
            proj = _in_proj(xf, norm_w3, mod, a_w_in, i, j, s)
            og = _dilated_attention(proj.reshape(b, s, -1), slopes)
            xf = _out_proj(og.reshape(b * s, -1), a_w_out, xf, mod, i, j, s)
        else:
            proj = _in_proj(xf, norm_w3, mod, b_w_in, i, j, s)
            og = _moba_attention(proj.reshape(b, s, -1), slopes)
            xf = _out_proj(og.reshape(b * s, -1), b_w_out, xf, mod, i, j, s)
    return _final_norm(xf, final_norm_w).reshape(b, s, d)
```

```python
import functools

import jax
import jax.numpy as jnp
from jax import lax
from jax.experimental import pallas as pl
from jax.experimental.pallas import tpu as pltpu

F32 = jnp.float32
BF16 = jnp.bfloat16

HEAD_DIM = 128
N_HEADS = 16
DSWA_PATTERNS = ((128, 1), (512, 4), (2048, 16))
N_GROUPS = len(DSWA_PATTERNS)
QB = 128
MOBA_BLOCK = 256
MOBA_TOPK = 3
EPS = 1e-6
NEG = -1e30
SM_SCALE = HEAD_DIM ** -0.5

VMEM_LIMIT_BYTES = 56 * 1024 * 1024
MOD_PAD_ROWS = 16


def _params(*sem):
    return pltpu.CompilerParams(dimension_semantics=sem, vmem_limit_bytes=VMEM_LIMIT_BYTES)


def _mod_kernel(c_ref, w_ref, b_ref, o_ref):
    c = c_ref[...]
    cond = c / (1.0 + jnp.exp(-c))
    acc = jnp.dot(cond.astype(BF16), w_ref[...].astype(BF16), preferred_element_type=F32)
    o_ref[...] = acc + b_ref[...]


def _modulation(c, mod_w, mod_b):
    depth, d, n = mod_w.shape
    b = c.shape[0]
    tn = n // 4
    c_pad = jnp.pad(c, ((0, MOD_PAD_ROWS - b), (0, 0)))
    out = pl.pallas_call(
        _mod_kernel,
        grid=(depth, n // tn),
        in_specs=[
            pl.BlockSpec((MOD_PAD_ROWS, d), lambda l, j: (0, 0)),
            pl.BlockSpec((None, d, tn), lambda l, j: (l, 0, j)),
            pl.BlockSpec((None, 1, tn), lambda l, j: (l, 0, j)),
        ],
        out_specs=pl.BlockSpec((None, MOD_PAD_ROWS, tn), lambda l, j: (l, 0, j)),
        out_shape=jax.ShapeDtypeStruct((depth, MOD_PAD_ROWS, n), F32),
        compiler_params=_params("parallel", "arbitrary"),
        name="adaln_modulation",
    )(c_pad, mod_w, mod_b.reshape(depth, 1, n))
    return out[:, :b].reshape(depth, b, 3, d)


def _inproj_kernel(x_ref, nw_ref, mod_ref, w_ref, o_ref, h_ref):
    @pl.when(pl.program_id(1) == 0)
    def _():
        x = x_ref[...]
        ms = jnp.mean(x * x, axis=-1, keepdims=True)
        y = x * lax.rsqrt(ms + EPS) * nw_ref[...]
        h = y * (1.0 + mod_ref[1:2, :]) + mod_ref[0:1, :]
        h_ref[...] = h.astype(BF16)

    acc = jnp.dot(h_ref[...], w_ref[...].astype(BF16), preferred_element_type=F32)
    o_ref[...] = acc.astype(o_ref.dtype)


def _in_proj(xf, norm_w3, mod, w_in, layer, widx, seq):
    m, d = xf.shape
    n = w_in.shape[-1]
    tm, tn = 1024, 1024
    tiles_per_batch = seq // tm
    return pl.pallas_call(
        _inproj_kernel,
        grid=(m // tm, n // tn),
        in_specs=[
            pl.BlockSpec((tm, d), lambda i, j: (i, 0)),
            pl.BlockSpec((None, 1, d), lambda i, j: (layer, 0, 0)),
            pl.BlockSpec((None, None, 3, d), lambda i, j: (layer, i // tiles_per_batch, 0, 0)),
            pl.BlockSpec((None, d, tn), lambda i, j: (widx, 0, j)),
        ],
        out_specs=pl.BlockSpec((tm, tn), lambda i, j: (i, j)),
        out_shape=jax.ShapeDtypeStruct((m, n), BF16),
        scratch_shapes=[pltpu.VMEM((tm, d), BF16)],
        compiler_params=_params("parallel", "arbitrary"),
        name=f"in_proj_l{layer}",
    )(xf, norm_w3, mod, w_in)


def _outproj_kernel(og_ref, w_ref, x_ref, mod_ref, o_ref):
    y = jnp.dot(og_ref[...], w_ref[...].astype(BF16), preferred_element_type=F32)
    o_ref[...] = x_ref[...] + mod_ref[2:3, :] * y


def _out_proj(og, w_out, xf, mod, layer, widx, seq):
    m, k = og.shape
    n = w_out.shape[-1]
    tm, tn = 1024, 1024
    tiles_per_batch = seq // tm
    return pl.pallas_call(
        _outproj_kernel,
        grid=(m // tm, n // tn),
        in_specs=[
            pl.BlockSpec((tm, k), lambda i, j: (i, 0)),
            pl.BlockSpec((None, k, tn), lambda i, j: (widx, 0, j)),
            pl.BlockSpec((tm, tn), lambda i, j: (i, j)),
            pl.BlockSpec((None, None, 3, tn), lambda i, j: (layer, i // tiles_per_batch, 0, j)),
        ],
        out_specs=pl.BlockSpec((tm, tn), lambda i, j: (i, j)),
        out_shape=jax.ShapeDtypeStruct((m, n), F32),
        compiler_params=_params("parallel", "arbitrary"),
        name=f"out_proj_l{layer}",
    )(og, w_out, xf, mod)


def _final_norm_kernel(x_ref, w_ref, o_ref):
    x = x_ref[...]
    ms = jnp.mean(x * x, axis=-1, keepdims=True)
    o_ref[...] = x * lax.rsqrt(ms + EPS) * w_ref[...]


def _final_norm(xf, w):
    m, d = xf.shape
    tm = 512
    return pl.pallas_call(
        _final_norm_kernel,
        grid=(m // tm,),
        in_specs=[pl.BlockSpec((tm, d), lambda i: (i, 0)), pl.BlockSpec((1, d), lambda i: (0, 0))],
        out_specs=pl.BlockSpec((tm, d), lambda i: (i, 0)),
        out_shape=jax.ShapeDtypeStruct((m, d), F32),
        compiler_params=_params("parallel"),
        name="final_norm",
    )(xf, w.reshape(1, d))


def _nt_dot(a, b):
    return lax.dot_general(a, b, (((1,), (1,)), ((), ())), preferred_element_type=F32)


def _band_bias(slope, dilation, with_prev):
    nk = 2 * QB if with_prev else QB
    qi = lax.broadcasted_iota(jnp.int32, (QB, nk), 0)
    ki = lax.broadcasted_iota(jnp.int32, (QB, nk), 1)
    step = qi - ki + (QB if with_prev else 0)
    ok = (step >= 0) & (step <= QB)
    return jnp.where(ok, (-slope * dilation) * step.astype(F32), NEG)


def _softmax_block(q, k, v, bias):
    s = _nt_dot(q, k) * SM_SCALE + bias
    m = jnp.max(s, axis=-1, keepdims=True)
    p = jnp.exp(s - m)
    l = jnp.sum(p, axis=-1, keepdims=True)
    acc = jnp.dot(p.astype(BF16), v, preferred_element_type=F32)
    return acc * (1.0 / l), m + jnp.log(l)


def _dilated_kernel(slopes_ref, q0, k0, v0, q1, k1, v1, q2, k2, v2, z_ref, o_ref,
                    sq1, sk1, sv1, sq2, sk2, sv2, so1, sl1, so2, sl2):
    slope = slopes_ref[pl.program_id(1)]

    for src, dst in ((q1, sq1), (k1, sk1), (v1, sv1), (q2, sq2), (k2, sk2), (v2, sv2)):
        dst[...] = src[...].astype(F32)

    def strided(ref, start, size, stride):
        return ref[pl.ds(start, size, stride=stride), :].astype(BF16)

    def store_block(o_scr, l_scr, start, stride, o, lse):
        o_scr[pl.ds(start, QB, stride=stride), :] = o
        l_scr[pl.ds(start, QB, stride=stride), :] = jnp.broadcast_to(lse, (QB, HEAD_DIM))

    d1 = DSWA_PATTERNS[1][1]
    nq1 = q1.shape[0] // d1 // QB
    bias1_c = _band_bias(slope, d1, False)
    bias1_pc = _band_bias(slope, d1, True)

    for r in range(d1):
        o, lse = _softmax_block(strided(sq1, r, QB, d1), strided(sk1, r, QB, d1),
                                strided(sv1, r, QB, d1), bias1_c)
        store_block(so1, sl1, r, d1, o, lse)
        for n in range(1, nq1):
            qs = r + n * QB * d1
            ks = r + (n - 1) * QB * d1
            o, lse = _softmax_block(strided(sq1, qs, QB, d1), strided(sk1, ks, 2 * QB, d1),
                                    strided(sv1, ks, 2 * QB, d1), bias1_pc)
            store_block(so1, sl1, qs, d1, o, lse)

    d2 = DSWA_PATTERNS[2][1]
    bias2_c = _band_bias(slope, d2, False)
    for r in range(d2):
        o, lse = _softmax_block(strided(sq2, r, QB, d2), strided(sk2, r, QB, d2),
                                strided(sv2, r, QB, d2), bias2_c)
        store_block(so2, sl2, r, d2, o, lse)

    d0 = DSWA_PATTERNS[0][1]
    bias0_c = _band_bias(slope, d0, False)
    bias0_pc = _band_bias(slope, d0, True)

    def finish(start, o0, lse0):
        rows = pl.ds(start, QB)
        l0 = jnp.broadcast_to(lse0, (QB, HEAD_DIM))
        l1 = sl1[rows, :]
        l2 = sl2[rows, :]
        mx = jnp.maximum(jnp.maximum(l0, l1), l2)
        w0 = jnp.exp(l0 - mx)
        w1 = jnp.exp(l1 - mx)
        w2 = jnp.exp(l2 - mx)
        o = (w0 * o0 + w1 * so1[rows, :] + w2 * so2[rows, :]) / (w0 + w1 + w2)
        z = z_ref[rows, :].astype(F32)
        o_ref[rows, :] = (o * (z / (1.0 + jnp.exp(-z)))).astype(o_ref.dtype)

    o0, lse0 = _softmax_block(q0[pl.ds(0, QB), :], k0[pl.ds(0, QB), :], v0[pl.ds(0, QB), :], bias0_c)
    finish(0, o0, lse0)

    for n in range(1, q0.shape[0] // QB):
        qs = n * QB
        ks = (n - 1) * QB
        o0, lse0 = _softmax_block(q0[pl.ds(qs, QB), :], k0[pl.ds(ks, 2 * QB), :],
                                  v0[pl.ds(ks, 2 * QB), :], bias0_pc)
        finish(qs, o0, lse0)


def _dilated_attention(proj, slopes):
    b, s, _ = proj.shape
    blk = (None, s, HEAD_DIM)

    def col(c):
        return pl.BlockSpec(blk, lambda bi, hi, sl: (bi, 0, c * N_HEADS + hi))

    grid_spec = pltpu.PrefetchScalarGridSpec(
        num_scalar_prefetch=1,
        grid=(b, N_HEADS),
        in_specs=[col(c) for c in range(3 * N_GROUPS + 1)],
        out_specs=pl.BlockSpec(blk, lambda bi, hi, sl: (bi, 0, hi)),
        scratch_shapes=[pltpu.VMEM((s, HEAD_DIM), F32) for _ in range(10)],
    )
    return pl.pallas_call(
        _dilated_kernel,
        grid_spec=grid_spec,
        out_shape=jax.ShapeDtypeStruct((b, s, N_HEADS * HEAD_DIM), BF16),
        compiler_params=_params("parallel", "arbitrary"),
        name="dilated_mixer",
    )(slopes, *([proj] * (3 * N_GROUPS + 1)))


def _moba_kernel(slopes_ref, q_ref, k_ref, v_ref, z_ref, o_ref, ka_ref):
    slope = slopes_ref[pl.program_id(1)]
    s_len = q_ref.shape[0]
    nblk = s_len // MOBA_BLOCK
    blk = MOBA_BLOCK

    row = lax.broadcasted_iota(jnp.int32, (HEAD_DIM, s_len), 0)
    pos = lax.broadcasted_iota(jnp.int32, (HEAD_DIM, s_len), 1)
    ind = jnp.where(pos // blk == row, 1.0 / blk, 0.0).astype(BF16)
    kmean = jnp.dot(ind, k_ref[...], preferred_element_type=F32)
    k_hi = kmean.astype(BF16)
    k_lo = (kmean - k_hi.astype(F32)).astype(BF16)
    q = q_ref[...]
    gate = _nt_dot(q, k_hi) + _nt_dot(q, k_lo)

    kblk = lax.broadcasted_iota(jnp.int32, (s_len, HEAD_DIM), 0) // blk
    klane = lax.broadcasted_iota(jnp.int32, (s_len, HEAD_DIM), 1)
    ka_ref[:, :HEAD_DIM] = k_ref[...]
    ka_ref[:, HEAD_DIM:] = jnp.where(klane == kblk, 1.0, 0.0).astype(BF16)

    tq = lax.broadcasted_iota(jnp.int32, (blk, blk), 0)
    sk = lax.broadcasted_iota(jnp.int32, (blk, blk), 1)
    own_bias = jnp.where(tq >= sk, slope * sk.astype(F32), NEG)
    key_pos = lax.broadcasted_iota(jnp.int32, (1, s_len), 1).astype(F32)
    lane = lax.broadcasted_iota(jnp.int32, (blk, HEAD_DIM), 1)

    for n in range(nblk):
        rows = pl.ds(n * blk, blk)
        q_n = q_ref[rows, :]
        s_own = _nt_dot(q_n, k_ref[rows, :]) * SM_SCALE + own_bias
        m = jnp.max(s_own, axis=-1, keepdims=True)
        if n > 0:
            g = gate[n * blk:(n + 1) * blk, :]
            rank = jnp.zeros((blk, HEAD_DIM), jnp.int32)
            for mp in range(n):
                g_mp = g[:, mp:mp + 1]
                beats = (g_mp > g) | ((g_mp == g) & (mp < lane))
                rank = rank + jnp.where(beats, 1, 0)
            selected = (lane < n) & (rank < MOBA_TOPK)
            qa = jnp.concatenate([q_n, jnp.where(selected, 0.0, NEG).astype(BF16)], axis=1)
            past_bias = slope * (key_pos[:, :n * blk] - float(n * blk))
            s_past = _nt_dot(qa, ka_ref[pl.ds(0, n * blk), :]) * SM_SCALE + past_bias
            m = jnp.maximum(m, jnp.max(s_past, axis=-1, keepdims=True))
            p = jnp.concatenate([jnp.exp(s_past - m), jnp.exp(s_own - m)], axis=1)
        else:
            p = jnp.exp(s_own - m)
        l = jnp.sum(p, axis=-1, keepdims=True)
        acc = jnp.dot(p.astype(BF16), v_ref[pl.ds(0, (n + 1) * blk), :], preferred_element_type=F32)
        z = z_ref[rows, :].astype(F32)
        o_ref[rows, :] = (acc * (1.0 / l) * (z / (1.0 + jnp.exp(-z)))).astype(o_ref.dtype)


def _moba_attention(proj, slopes):
    b, s, _ = proj.shape
    blk = (None, s, HEAD_DIM)

    def col(c):
        return pl.BlockSpec(blk, lambda bi, hi, sl: (bi, 0, c * N_HEADS + hi))

    grid_spec = pltpu.PrefetchScalarGridSpec(
        num_scalar_prefetch=1,
        grid=(b, N_HEADS),
        in_specs=[col(c) for c in range(4)],
        out_specs=pl.BlockSpec(blk, lambda bi, hi, sl: (bi, 0, hi)),
        scratch_shapes=[pltpu.VMEM((s, 2 * HEAD_DIM), BF16)],
    )
    return pl.pallas_call(
        _moba_kernel,
        grid_spec=grid_spec,
        out_shape=jax.ShapeDtypeStruct((b, s, N_HEADS * HEAD_DIM), BF16),
        compiler_params=_params("parallel", "arbitrary"),
        name="moba_mixer",
    )(slopes, proj, proj, proj, proj)


def kernel(x, c, norm_w, mod_w, mod_b, a_w_in, a_w_out, b_w_in, b_w_out, final_norm_w):
    b, s, d = x.shape
    depth = norm_w.shape[0]
    slopes = jnp.exp2(-8.0 * jnp.arange(1, N_HEADS + 1, dtype=F32) / N_HEADS)
    mod = _modulation(c, mod_w, mod_b)
    norm_w3 = norm_w.reshape(depth, 1, d)
    xf = x.reshape(b * s, d)
    for i in range(depth):
        j = i // 2
        if i % 2 == 0:
            proj = _in_proj(xf, norm_w3, mod, a_w_in, i, j, s)
            og = _dilated_attention(proj.reshape(b, s, -1), slopes)
            xf = _out_proj(og.reshape(b * s, -1), a_w_out, xf, mod, i, j, s)
        else:
            proj = _in_proj(xf, norm_w3, mod, b_w_in, i, j, s)
            og = _moba_attention(proj.reshape(b, s, -1), slopes)
            xf = _out_proj(og.reshape(b * s, -1), b_w_out, xf, mod, i, j, s)
    return _final_norm(xf, final_norm_w).reshape(b, s, d)
```

```python
import math

import jax
import jax.numpy as jnp
from jax import lax
from jax.experimental import pallas as pl
from jax.experimental.pallas import tpu as pltpu

F32 = jnp.float32
BF16 = jnp.bfloat16

HEAD_DIM = 128
N_HEADS = 16
DSWA_PATTERNS = ((128, 1), (512, 4), (2048, 16))
DILATIONS = tuple(d for _, d in DSWA_PATTERNS)
N_GROUPS = len(DSWA_PATTERNS)
QB = 128
MOBA_BLOCK = 256
MOBA_TOPK = 3
EPS = 1e-6
NEG = -1e30
SM_SCALE = HEAD_DIM ** -0.5
EXP2_SCALE = SM_SCALE * math.log2(math.e)

LANES = 128
VMEM_LIMIT_BYTES = 56 * 1024 * 1024
MOD_PAD_ROWS = 16
PROJ_TM = 1024
PROJ_TN = 1024


def _params(*sem):
    return pltpu.CompilerParams(dimension_semantics=sem, vmem_limit_bytes=VMEM_LIMIT_BYTES)


def _silu(z):
    return z / (1.0 + jnp.exp(-z))


def _mod_kernel(c_ref, w_ref, b_ref, o_ref):
    cond = _silu(c_ref[...])
    acc = jnp.dot(cond.astype(BF16), w_ref[...].astype(BF16), preferred_element_type=F32)
    o_ref[...] = acc + b_ref[...]


def _modulation(c, mod_w, mod_b):
    depth, d, n = mod_w.shape
    b = c.shape[0]
    tn = n // 4
    c_pad = jnp.pad(c, ((0, MOD_PAD_ROWS - b), (0, 0)))
    out = pl.pallas_call(
        _mod_kernel,
        grid=(depth, n // tn),
        in_specs=[
            pl.BlockSpec((MOD_PAD_ROWS, d), lambda l, j: (0, 0)),
            pl.BlockSpec((None, d, tn), lambda l, j: (l, 0, j)),
            pl.BlockSpec((None, 1, tn), lambda l, j: (l, 0, j)),
        ],
        out_specs=pl.BlockSpec((None, MOD_PAD_ROWS, tn), lambda l, j: (l, 0, j)),
        out_shape=jax.ShapeDtypeStruct((depth, MOD_PAD_ROWS, n), F32),
        compiler_params=_params("parallel", "arbitrary"),
        name="adaln_modulation",
    )(c_pad, mod_w, mod_b.reshape(depth, 1, n))
    return out[:, :b].reshape(depth, b, 3, d)


def _inproj_kernel(x_ref, nw_ref, mod_ref, w_ref, o_ref, h_ref, *stage, dilations, tiles_per_copy):
    j = pl.program_id(1)
    tm, d_model = x_ref.shape

    @pl.when(j == 0)
    def _():
        x = x_ref[...]
        rs = lax.rsqrt(jnp.mean(x * x, axis=-1, keepdims=True) + EPS)
        for c in range(d_model // LANES):
            cs = slice(c * LANES, (c + 1) * LANES)
            y = x_ref[:, cs] * rs * nw_ref[:, cs]
            hc = y * (1.0 + mod_ref[1:2, cs]) + mod_ref[0:1, cs]
            h_ref[0, :, cs] = hc.astype(BF16)
            if len(dilations) > 1:
                st = stage[0].at[c % 2]
                st[...] = hc
                for g, dil in enumerate(dilations[1:], 1):
                    rows = tm // dil
                    for r in range(dil):
                        h_ref[g, r * rows:(r + 1) * rows, cs] = (
                            st[pl.ds(r, rows, stride=dil), :].astype(BF16))

    if len(dilations) > 1:
        n_copies = len(dilations)
        copy = jnp.where(j < tiles_per_copy * n_copies, j // tiles_per_copy, 0)
        h = h_ref[copy]
    else:
        h = h_ref[0]
    acc = jnp.dot(h, w_ref[...].astype(BF16), preferred_element_type=F32)
    o_ref[...] = acc.astype(o_ref.dtype)


def _in_proj(xf, norm_w3, mod, w_in, layer, widx, seq, dilations):
    m, d = xf.shape
    n = w_in.shape[-1]
    tm, tn = PROJ_TM, PROJ_TN
    tiles_per_batch = seq // tm
    n_copies = len(dilations)
    tiles_per_copy = 3 * N_HEADS * HEAD_DIM // tn
    scratch = [pltpu.VMEM((n_copies, tm, d), BF16)]
    x_spec = pl.BlockSpec((tm, d), lambda i, j: (i, 0))
    if n_copies > 1:
        scratch.append(pltpu.VMEM((2, tm, LANES), F32))
        x_spec = pl.BlockSpec((tm, d), lambda i, j: (i, 0), pipeline_mode=pl.Buffered(1))
    kern = lambda *refs: _inproj_kernel(*refs, dilations=dilations, tiles_per_copy=tiles_per_copy)
    return pl.pallas_call(
        kern,
        grid=(m // tm, n // tn),
        in_specs=[
            x_spec,
            pl.BlockSpec((None, 1, d), lambda i, j: (layer, 0, 0)),
            pl.BlockSpec((None, None, 3, d), lambda i, j: (layer, i // tiles_per_batch, 0, 0)),
            pl.BlockSpec((None, d, tn), lambda i, j: (widx, 0, j)),
        ],
        out_specs=pl.BlockSpec((tm, tn), lambda i, j: (i, j)),
        out_shape=jax.ShapeDtypeStruct((m, n), BF16),
        scratch_shapes=scratch,
        compiler_params=_params("parallel", "arbitrary"),
        name=f"in_proj_l{layer}",
    )(xf, norm_w3, mod, w_in)


def _outproj_kernel(og_ref, w_ref, x_ref, mod_ref, o_ref):
    y = jnp.dot(og_ref[...], w_ref[...].astype(BF16), preferred_element_type=F32)
    o_ref[...] = x_ref[...] + mod_ref[2:3, :] * y


def _out_proj(og, w_out, xf, mod, layer, widx, seq):
    m, k = og.shape
    n = w_out.shape[-1]
    tm, tn = PROJ_TM, PROJ_TN
    tiles_per_batch = seq // tm
    return pl.pallas_call(
        _outproj_kernel,
        grid=(m // tm, n // tn),
        in_specs=[
            pl.BlockSpec((tm, k), lambda i, j: (i, 0)),
            pl.BlockSpec((None, k, tn), lambda i, j: (widx, 0, j)),
            pl.BlockSpec((tm, tn), lambda i, j: (i, j)),
            pl.BlockSpec((None, None, 3, tn), lambda i, j: (layer, i // tiles_per_batch, 0, j)),
        ],
        out_specs=pl.BlockSpec((tm, tn), lambda i, j: (i, j)),
        out_shape=jax.ShapeDtypeStruct((m, n), F32),
        compiler_params=_params("parallel", "arbitrary"),
        name=f"out_proj_l{layer}",
    )(og, w_out, xf, mod)


def _final_norm_kernel(x_ref, w_ref, o_ref):
    x = x_ref[...]
    ms = jnp.mean(x * x, axis=-1, keepdims=True)
    o_ref[...] = x * lax.rsqrt(ms + EPS) * w_ref[...]


def _final_norm(xf, w):
    m, d = xf.shape
    tm = 512
    return pl.pallas_call(
        _final_norm_kernel,
        grid=(m // tm,),
        in_specs=[pl.BlockSpec((tm, d), lambda i: (i, 0)), pl.BlockSpec((1, d), lambda i: (0, 0))],
        out_specs=pl.BlockSpec((tm, d), lambda i: (i, 0)),
        out_shape=jax.ShapeDtypeStruct((m, d), F32),
        compiler_params=_params("parallel"),
        name="final_norm",
    )(xf, w.reshape(1, d))


def _nt_dot(a, b):
    return lax.dot_general(a, b, (((1,), (1,)), ((), ())), preferred_element_type=F32)


def _exp_weights(d, m):
    return jnp.exp2((d - m) * EXP2_SCALE)


def _pv_with_denominator(p, v):
    v1 = jnp.concatenate([v, jnp.ones(v.shape, v.dtype)], axis=1)
    acc = jnp.dot(p.astype(BF16), v1, preferred_element_type=F32)
    return acc[:, :HEAD_DIM], acc[:, HEAD_DIM:]


def _band_bias(slope, dilation, with_prev):
    nk = 2 * QB if with_prev else QB
    qi = lax.broadcasted_iota(jnp.int32, (QB, nk), 0)
    ki = lax.broadcasted_iota(jnp.int32, (QB, nk), 1)
    step = qi - ki + (QB if with_prev else 0)
    ok = (step >= 0) & (step <= QB)
    return jnp.where(ok, (-slope * dilation / SM_SCALE) * step.astype(F32), NEG)


def _softmax_block(q, k, v, bias):
    d = _nt_dot(q, k) + bias
    m = jnp.max(d, axis=-1, keepdims=True)
    acc, l = _pv_with_denominator(_exp_weights(d, m), v)
    return acc / l, m * SM_SCALE + jnp.log(l)


def _dilated_kernel(slopes_ref, q0, k0, v0, q1, k1, v1, q2, k2, v2, z_ref, o_ref,
                    so1, sl1, so2, sl2, to2, tl2, *, tile_rows):
    slope = slopes_ref[pl.program_id(1)]
    s_len = q0.shape[0]
    d0, d1, d2 = DILATIONS

    def rows(ref, start, size=QB):
        return ref[pl.ds(start, size), :]

    per_res1 = tile_rows // d1
    nq1 = s_len // d1 // QB
    bias1_c = _band_bias(slope, d1, False)
    bias1_pc = _band_bias(slope, d1, True)

    def start1(r, n):
        l0 = n * QB
        return (l0 // per_res1) * tile_rows + r * per_res1 + l0 % per_res1

    for r in range(d1):
        for n in range(nq1):
            q = rows(q1, start1(r, n))
            if n == 0:
                k, v, bias = rows(k1, start1(r, 0)), rows(v1, start1(r, 0)), bias1_c
            else:
                k = jnp.concatenate([rows(k1, start1(r, n - 1)), rows(k1, start1(r, n))], axis=0)
                v = jnp.concatenate([rows(v1, start1(r, n - 1)), rows(v1, start1(r, n))], axis=0)
                bias = bias1_pc
            o, lse = _softmax_block(q, k, v, bias)
            dst = pl.ds(n * QB * d1 + r, QB, stride=d1)
            so1[dst, :] = o
            sl1[dst, :] = lse

    per_res2 = tile_rows // d2
    n_tiles = s_len // tile_rows
    bias2_c = _band_bias(slope, d2, False)

    def gather2(ref, r):
        return jnp.concatenate([rows(ref, t * tile_rows + r * per_res2, per_res2)
                                for t in range(n_tiles)], axis=0)

    for r in range(d2):
        o, lse = _softmax_block(gather2(q2, r), gather2(k2, r), gather2(v2, r), bias2_c)
        dst = pl.ds((r % d1) * (s_len // d1) + r // d1, QB, stride=d2 // d1)
        to2[dst, :] = o
        tl2[dst, :] = lse
    for r in range(d1):
        src = pl.ds(r * (s_len // d1), s_len // d1)
        dst = pl.ds(r, s_len // d1, stride=d1)
        so2[dst, :] = to2[src, :]
        sl2[dst, :] = tl2[src, :]

    bias0_c = _band_bias(slope, d0, False)
    bias0_pc = _band_bias(slope, d0, True)
    for n in range(s_len // QB):
        if n == 0:
            o0, l0 = _softmax_block(rows(q0, 0), rows(k0, 0), rows(v0, 0), bias0_c)
        else:
            o0, l0 = _softmax_block(rows(q0, n * QB), rows(k0, (n - 1) * QB, 2 * QB),
                                    rows(v0, (n - 1) * QB, 2 * QB), bias0_pc)
        blk = pl.ds(n * QB, QB)
        l1 = sl1[blk, :]
        l2 = sl2[blk, :]
        mx = jnp.maximum(jnp.maximum(l0, l1), l2)
        w0 = jnp.exp(l0 - mx)
        w1 = jnp.exp(l1 - mx)
        w2 = jnp.exp(l2 - mx)
        o = (w0 * o0 + w1 * so1[blk, :] + w2 * so2[blk, :]) / (w0 + w1 + w2)
        o_ref[blk, :] = (o * _silu(z_ref[blk, :].astype(F32))).astype(o_ref.dtype)


def _dilated_attention(proj, slopes, tile_rows):
    b, s, _ = proj.shape
    blk = (None, s, HEAD_DIM)

    def col(c):
        return pl.BlockSpec(blk, lambda bi, hi, sl: (bi, 0, c * N_HEADS + hi))

    grid_spec = pltpu.PrefetchScalarGridSpec(
        num_scalar_prefetch=1,
        grid=(b, N_HEADS),
        in_specs=[col(c) for c in range(3 * N_GROUPS + 1)],
        out_specs=pl.BlockSpec(blk, lambda bi, hi, sl: (bi, 0, hi)),
        scratch_shapes=[pltpu.VMEM((s, HEAD_DIM), F32) for _ in range(6)],
    )
    kern = lambda *refs: _dilated_kernel(*refs, tile_rows=tile_rows)
    return pl.pallas_call(
        kern,
        grid_spec=grid_spec,
        out_shape=jax.ShapeDtypeStruct((b, s, N_HEADS * HEAD_DIM), BF16),
        compiler_params=_params("parallel", "arbitrary"),
        name="dilated_mixer",
    )(slopes, *([proj] * (3 * N_GROUPS + 1)))


N_BIAS_PARTS = 3


def _moba_kernel(slopes_ref, q_ref, k_ref, v_ref, z_ref, o_ref, ka_ref, va_ref):
    slope = slopes_ref[pl.program_id(0)]
    s_len = q_ref.shape[0]
    blk = MOBA_BLOCK
    nblk = s_len // blk

    @pl.when(pl.program_id(1) == 0)
    def _():
        pos = lax.broadcasted_iota(jnp.int32, (s_len, HEAD_DIM), 0)
        lane = lax.broadcasted_iota(jnp.int32, (s_len, HEAD_DIM), 1)
        bias = pos.astype(F32) * (slope / SM_SCALE)
        aug = jnp.where(lane == pos // blk, 1.0, 0.0)
        for part in range(N_BIAS_PARTS):
            piece = bias.astype(BF16).astype(F32)
            aug = jnp.where(lane == nblk + part, piece, aug)
            bias = bias - piece
        ka_ref[:, HEAD_DIM:] = aug.astype(BF16)
        va_ref[:, HEAD_DIM:] = jnp.ones((s_len, HEAD_DIM), BF16)

    ka_ref[:, :HEAD_DIM] = k_ref[...]
    va_ref[:, :HEAD_DIM] = v_ref[...]

    row = lax.broadcasted_iota(jnp.int32, (HEAD_DIM, s_len), 0)
    col = lax.broadcasted_iota(jnp.int32, (HEAD_DIM, s_len), 1)
    ind = jnp.where(col // blk == row, 1.0 / blk, 0.0).astype(BF16)
    kmean = jnp.dot(ind, k_ref[...], preferred_element_type=F32)
    k_hi = kmean.astype(BF16)
    k_lo = (kmean - k_hi.astype(F32)).astype(BF16)
    q = q_ref[...]
    gate_t = (_nt_dot(k_hi, q) + _nt_dot(k_lo, q))[:nblk, :]

    sub = lax.broadcasted_iota(jnp.int32, (nblk, blk), 0)
    tq = lax.broadcasted_iota(jnp.int32, (blk, blk), 0)
    sk = lax.broadcasted_iota(jnp.int32, (blk, blk), 1)
    causal = jnp.where(tq >= sk, 0.0, NEG)
    aug_row = lax.broadcasted_iota(jnp.int32, (HEAD_DIM, blk), 0)
    unit_rows = jnp.where((aug_row >= nblk) & (aug_row < nblk + N_BIAS_PARTS), 1.0, 0.0)
    pad_rows = jnp.zeros((HEAD_DIM - nblk, blk), F32)

    for n in range(nblk):
        rows = pl.ds(n * blk, blk)
        g = gate_t[:, n * blk:(n + 1) * blk]
        rank = jnp.zeros((nblk, blk), jnp.int32)
        for mp in range(n):
            g_mp = g[mp:mp + 1, :]
            beats = (g_mp > g) | ((g_mp == g) & (mp < sub))
            rank = rank + jnp.where(beats, 1, 0)
        allowed = ((sub < n) & (rank < MOBA_TOPK)) | (sub == n)
        mask_t = jnp.concatenate([jnp.where(allowed, 0.0, NEG), pad_rows], axis=0) + unit_rows
        qa = jnp.concatenate([q_ref[rows, :], mask_t.T.astype(BF16)], axis=1)

        nk = (n + 1) * blk
        d = _nt_dot(qa, ka_ref[pl.ds(0, nk), :])
        d_own = d[:, n * blk:] + causal
        m = jnp.max(d_own, axis=-1, keepdims=True)
        if n > 0:
            d_past = d[:, :n * blk]
            m = jnp.maximum(m, jnp.max(d_past, axis=-1, keepdims=True))
            p = jnp.concatenate([_exp_weights(d_past, m), _exp_weights(d_own, m)], axis=1)
        else:
            p = _exp_weights(d_own, m)
        acc = jnp.dot(p.astype(BF16), va_ref[pl.ds(0, nk), :], preferred_element_type=F32)
        o = acc[:, :HEAD_DIM] / acc[:, HEAD_DIM:]
        o_ref[rows, :] = (o * _silu(z_ref[rows, :].astype(F32))).astype(o_ref.dtype)


def _moba_attention(proj, slopes):
    b, s, _ = proj.shape
    blk = (None, s, HEAD_DIM)

    def col(c):
        return pl.BlockSpec(blk, lambda hi, bi, sl: (bi, 0, c * N_HEADS + hi))

    grid_spec = pltpu.PrefetchScalarGridSpec(
        num_scalar_prefetch=1,
        grid=(N_HEADS, b),
        in_specs=[col(c) for c in range(4)],
        out_specs=pl.BlockSpec(blk, lambda hi, bi, sl: (bi, 0, hi)),
        scratch_shapes=[pltpu.VMEM((s, 2 * HEAD_DIM), BF16), pltpu.VMEM((s, 2 * HEAD_DIM), BF16)],
    )
    return pl.pallas_call(
        _moba_kernel,
        grid_spec=grid_spec,
        out_shape=jax.ShapeDtypeStruct((b, s, N_HEADS * HEAD_DIM), BF16),
        compiler_params=_params("arbitrary", "arbitrary"),
        name="moba_mixer",
    )(slopes, proj, proj, proj, proj)


def kernel(x, c, norm_w, mod_w, mod_b, a_w_in, a_w_out, b_w_in, b_w_out, final_norm_w):
    b, s, d = x.shape
    depth = norm_w.shape[0]
    slopes = jnp.exp2(-8.0 * jnp.arange(1, N_HEADS + 1, dtype=F32) / N_HEADS)
    mod = _modulation(c, mod_w, mod_b)
    norm_w3 = norm_w.reshape(depth, 1, d)
    xf = x.reshape(b * s, d)
    for i in range(depth):
        j = i // 2
        if i % 2 == 0:
            proj = _in_proj(xf, norm_w3, mod, a_w_in, i, j, s, DILATIONS)
            og = _dilated_attention(proj.reshape(b, s, -1), slopes, PROJ_TM)
            xf = _out_proj(og.reshape(b * s, -1), a_w_out, xf, mod, i, j, s)
        else:
            proj = _in_proj(xf, norm_w3, mod, b_w_in, i, j, s, (1,))
            og = _moba_attention(proj.reshape(b, s, -1), slopes)
            xf = _out_proj(og.reshape(b * s, -1), b_w_out, xf, mod, i, j, s)
    return _final_norm(xf, final_norm_w).reshape(b, s, d)
```

```python
import math

import jax
import jax.numpy as jnp
from jax import lax
from jax.experimental import pallas as pl
from jax.experimental.pallas import tpu as pltpu

F32 = jnp.float32
BF16 = jnp.bfloat16

HEAD_DIM = 128
N_HEADS = 16
DSWA_PATTERNS = ((128, 1), (512, 4), (2048, 16))
DILATIONS = tuple(d for _, d in DSWA_PATTERNS)
N_GROUPS = len(DSWA_PATTERNS)
QB = 128
MOBA_BLOCK = 256
MOBA_TOPK = 3
EPS = 1e-6
NEG = -1e30
SM_SCALE = HEAD_DIM ** -0.5
EXP2_SCALE = SM_SCALE * math.log2(math.e)

LANES = 128
VMEM_LIMIT_BYTES = 56 * 1024 * 1024
MOD_PAD_ROWS = 16
PROJ_TM = 1024
PROJ_TN = 1024


def _params(*sem):
    return pltpu.CompilerParams(dimension_semantics=sem, vmem_limit_bytes=VMEM_LIMIT_BYTES)


def _silu(z):
    return z / (1.0 + jnp.exp(-z))


def _mod_kernel(c_ref, w_ref, b_ref, o_ref):
    cond = _silu(c_ref[...])
    acc = jnp.dot(cond.astype(BF16), w_ref[...].astype(BF16), preferred_element_type=F32)
    o_ref[...] = acc + b_ref[...]


def _modulation(c, mod_w, mod_b):
    depth, d, n = mod_w.shape
    b = c.shape[0]
    tn = n // 4
    c_pad = jnp.pad(c, ((0, MOD_PAD_ROWS - b), (0, 0)))
    out = pl.pallas_call(
        _mod_kernel,
        grid=(depth, n // tn),
        in_specs=[
            pl.BlockSpec((MOD_PAD_ROWS, d), lambda l, j: (0, 0)),
            pl.BlockSpec((None, d, tn), lambda l, j: (l, 0, j)),
            pl.BlockSpec((None, 1, tn), lambda l, j: (l, 0, j)),
        ],
        out_specs=pl.BlockSpec((None, MOD_PAD_ROWS, tn), lambda l, j: (l, 0, j)),
        out_shape=jax.ShapeDtypeStruct((depth, MOD_PAD_ROWS, n), F32),
        compiler_params=_params("parallel", "arbitrary"),
        name="adaln_modulation",
    )(c_pad, mod_w, mod_b.reshape(depth, 1, n))
    return out[:, :b].reshape(depth, b, 3, d)


def _modulate_kernel(x_ref, nw_ref, mod_ref, h_ref, *stage, dilations):
    tm, d_model = x_ref.shape
    x = x_ref[...]
    rs = lax.rsqrt(jnp.mean(x * x, axis=-1, keepdims=True) + EPS)
    for c in range(d_model // LANES):
        cs = slice(c * LANES, (c + 1) * LANES)
        y = x_ref[:, cs] * rs * nw_ref[:, cs]
        hc = y * (1.0 + mod_ref[1:2, cs]) + mod_ref[0:1, cs]
        h_ref[0, :, cs] = hc.astype(BF16)
        if len(dilations) > 1:
            st = stage[0].at[c % 2]
            st[...] = hc
            for g, dil in enumerate(dilations[1:], 1):
                rows = tm // dil
                for r in range(dil):
                    h_ref[g, r * rows:(r + 1) * rows, cs] = (
                        st[pl.ds(r, rows, stride=dil), :].astype(BF16))


def _modulate(xf, norm_w3, mod, layer, seq, dilations):
    m, d = xf.shape
    tm = PROJ_TM
    tiles_per_batch = seq // tm
    n_copies = len(dilations)
    scratch = [pltpu.VMEM((2, tm, LANES), F32)] if n_copies > 1 else []
    kern = lambda *refs: _modulate_kernel(*refs, dilations=dilations)
    return pl.pallas_call(
        kern,
        grid=(m // tm,),
        in_specs=[
            pl.BlockSpec((tm, d), lambda i: (i, 0)),
            pl.BlockSpec((None, 1, d), lambda i: (layer, 0, 0)),
            pl.BlockSpec((None, None, 3, d), lambda i: (layer, i // tiles_per_batch, 0, 0)),
        ],
        out_specs=pl.BlockSpec((n_copies, tm, d), lambda i: (0, i, 0)),
        out_shape=jax.ShapeDtypeStruct((n_copies, m, d), BF16),
        scratch_shapes=scratch,
        compiler_params=_params("parallel"),
        name=f"modulate_l{layer}",
    )(xf, norm_w3, mod)


def _inproj_kernel(h_ref, w_ref, o_ref, wb_ref):
    @pl.when(pl.program_id(1) == 0)
    def _():
        wb_ref[...] = w_ref[...].astype(BF16)

    acc = jnp.dot(h_ref[...], wb_ref[...], preferred_element_type=F32)
    o_ref[...] = acc.astype(o_ref.dtype)


def _in_proj(h, w_in, layer, widx):
    n_copies, m, d = h.shape
    n = w_in.shape[-1]
    tm, tn = PROJ_TM, PROJ_TN
    tiles_per_copy = 3 * N_HEADS * HEAD_DIM // tn

    def copy_of(j):
        if n_copies == 1:
            return 0
        return jnp.where(j < tiles_per_copy * n_copies, j // tiles_per_copy, 0)

    return pl.pallas_call(
        _inproj_kernel,
        grid=(n // tn, m // tm),
        in_specs=[
            pl.BlockSpec((None, tm, d), lambda j, i: (copy_of(j), i, 0)),
            pl.BlockSpec((None, d, tn), lambda j, i: (widx, 0, j)),
        ],
        out_specs=pl.BlockSpec((tm, tn), lambda j, i: (i, j)),
        out_shape=jax.ShapeDtypeStruct((m, n), BF16),
        scratch_shapes=[pltpu.VMEM((d, tn), BF16)],
        compiler_params=_params("arbitrary", "arbitrary"),
        name=f"in_proj_l{layer}",
    )(h, w_in)


def _outproj_kernel(og_ref, w_ref, x_ref, mod_ref, *rest, final):
    if final:
        fw_ref, o_ref, wb_ref = rest
    else:
        o_ref, wb_ref = rest

    @pl.when(pl.program_id(1) == 0)
    def _():
        wb_ref[...] = w_ref[...].astype(BF16)

    y = jnp.dot(og_ref[...], wb_ref[...], preferred_element_type=F32)
    xn = x_ref[...] + mod_ref[2:3, :] * y
    if final:
        ms = jnp.mean(xn * xn, axis=-1, keepdims=True)
        xn = xn * lax.rsqrt(ms + EPS) * fw_ref[...]
    o_ref[...] = xn


def _out_proj(og, w_out, xf, mod, layer, widx, seq, final_w=None):
    m, k = og.shape
    n = w_out.shape[-1]
    final = final_w is not None
    tm, tn = (PROJ_TM // 2, n) if final else (PROJ_TM, PROJ_TN)
    tiles_per_batch = seq // tm
    w_mode = dict(pipeline_mode=pl.Buffered(1)) if final else {}
    in_specs = [
        pl.BlockSpec((tm, k), lambda j, i: (i, 0)),
        pl.BlockSpec((None, k, tn), lambda j, i: (widx, 0, j), **w_mode),
        pl.BlockSpec((tm, tn), lambda j, i: (i, j)),
        pl.BlockSpec((None, None, 3, tn), lambda j, i: (layer, i // tiles_per_batch, 0, j)),
    ]
    args = [og, w_out, xf, mod]
    if final:
        in_specs.append(pl.BlockSpec((1, n), lambda j, i: (0, 0)))
        args.append(final_w.reshape(1, n))
    kern = lambda *refs: _outproj_kernel(*refs, final=final)
    return pl.pallas_call(
        kern,
        grid=(n // tn, m // tm),
        in_specs=in_specs,
        out_specs=pl.BlockSpec((tm, tn), lambda j, i: (i, j)),
        out_shape=jax.ShapeDtypeStruct((m, n), F32),
        scratch_shapes=[pltpu.VMEM((k, tn), BF16)],
        compiler_params=_params("arbitrary", "arbitrary"),
        name=f"out_proj_l{layer}",
    )(*args)


def _nt_dot(a, b):
    return lax.dot_general(a, b, (((1,), (1,)), ((), ())), preferred_element_type=F32)


def _exp_weights(d, m):
    return jnp.exp2((d - m) * EXP2_SCALE)


def _pv_with_denominator(p, v):
    v1 = jnp.concatenate([v, jnp.ones(v.shape, v.dtype)], axis=1)
    acc = jnp.dot(p.astype(BF16), v1, preferred_element_type=F32)
    return acc[:, :HEAD_DIM], acc[:, HEAD_DIM:]


def _band_bias(slope, dilation, with_prev):
    nk = 2 * QB if with_prev else QB
    qi = lax.broadcasted_iota(jnp.int32, (QB, nk), 0)
    ki = lax.broadcasted_iota(jnp.int32, (QB, nk), 1)
    step = qi - ki + (QB if with_prev else 0)
    ok = (step >= 0) & (step <= QB)
    return jnp.where(ok, (-slope * dilation / SM_SCALE) * step.astype(F32), NEG)


def _softmax_block(q, k, v, bias):
    d = _nt_dot(q, k) + bias
    m = jnp.max(d, axis=-1, keepdims=True)
    acc, l = _pv_with_denominator(_exp_weights(d, m), v)
    return acc / l, m * SM_SCALE + jnp.log(l)


def _dilated_kernel(slopes_ref, q0, k0, v0, q1, k1, v1, q2, k2, v2, z_ref, o_ref,
                    so1, sl1, so2, sl2, to2, tl2, *, tile_rows):
    slope = slopes_ref[pl.program_id(1)]
    s_len = q0.shape[0]
    d0, d1, d2 = DILATIONS

    def rows(ref, start, size=QB):
        return ref[pl.ds(start, size), :]

    per_res1 = tile_rows // d1
    nq1 = s_len // d1 // QB
    bias1_c = _band_bias(slope, d1, False)
    bias1_pc = _band_bias(slope, d1, True)

    def start1(r, n):
        l0 = n * QB
        return (l0 // per_res1) * tile_rows + r * per_res1 + l0 % per_res1

    for r in range(d1):
        for n in range(nq1):
            q = rows(q1, start1(r, n))
            if n == 0:
                k, v, bias = rows(k1, start1(r, 0)), rows(v1, start1(r, 0)), bias1_c
            else:
                k = jnp.concatenate([rows(k1, start1(r, n - 1)), rows(k1, start1(r, n))], axis=0)
                v = jnp.concatenate([rows(v1, start1(r, n - 1)), rows(v1, start1(r, n))], axis=0)
                bias = bias1_pc
            o, lse = _softmax_block(q, k, v, bias)
            dst = pl.ds(n * QB * d1 + r, QB, stride=d1)
            so1[dst, :] = o
            sl1[dst, :] = lse

    per_res2 = tile_rows // d2
    n_tiles = s_len // tile_rows
    bias2_c = _band_bias(slope, d2, False)

    def gather2(ref, r):
        return jnp.concatenate([rows(ref, t * tile_rows + r * per_res2, per_res2)
                                for t in range(n_tiles)], axis=0)

    for r in range(d2):
        o, lse = _softmax_block(gather2(q2, r), gather2(k2, r), gather2(v2, r), bias2_c)
        dst = pl.ds((r % d1) * (s_len // d1) + r // d1, QB, stride=d2 // d1)
        to2[dst, :] = o
        tl2[dst, :] = lse
    for r in range(d1):
        src = pl.ds(r * (s_len // d1), s_len // d1)
        dst = pl.ds(r, s_len // d1, stride=d1)
        so2[dst, :] = to2[src, :]
        sl2[dst, :] = tl2[src, :]

    bias0_c = _band_bias(slope, d0, False)
    bias0_pc = _band_bias(slope, d0, True)
    for n in range(s_len // QB):
        if n == 0:
            o0, l0 = _softmax_block(rows(q0, 0), rows(k0, 0), rows(v0, 0), bias0_c)
        else:
            o0, l0 = _softmax_block(rows(q0, n * QB), rows(k0, (n - 1) * QB, 2 * QB),
                                    rows(v0, (n - 1) * QB, 2 * QB), bias0_pc)
        blk = pl.ds(n * QB, QB)
        l1 = sl1[blk, :]
        l2 = sl2[blk, :]
        mx = jnp.maximum(jnp.maximum(l0, l1), l2)
        w0 = jnp.exp(l0 - mx)
        w1 = jnp.exp(l1 - mx)
        w2 = jnp.exp(l2 - mx)
        o = (w0 * o0 + w1 * so1[blk, :] + w2 * so2[blk, :]) / (w0 + w1 + w2)
        o_ref[blk, :] = (o * _silu(z_ref[blk, :].astype(F32))).astype(o_ref.dtype)


def _dilated_attention(proj, slopes, tile_rows):
    b, s, _ = proj.shape
    blk = (None, s, HEAD_DIM)

    def col(c):
        return pl.BlockSpec(blk, lambda bi, hi, sl: (bi, 0, c * N_HEADS + hi))

    grid_spec = pltpu.PrefetchScalarGridSpec(
        num_scalar_prefetch=1,
        grid=(b, N_HEADS),
        in_specs=[col(c) for c in range(3 * N_GROUPS + 1)],
        out_specs=pl.BlockSpec(blk, lambda bi, hi, sl: (bi, 0, hi)),
        scratch_shapes=[pltpu.VMEM((s, HEAD_DIM), F32) for _ in range(6)],
    )
    kern = lambda *refs: _dilated_kernel(*refs, tile_rows=tile_rows)
    return pl.pallas_call(
        kern,
        grid_spec=grid_spec,
        out_shape=jax.ShapeDtypeStruct((b, s, N_HEADS * HEAD_DIM), BF16),
        compiler_params=_params("parallel", "arbitrary"),
        name="dilated_mixer",
    )(slopes, *([proj] * (3 * N_GROUPS + 1)))


N_BIAS_PARTS = 3


def _moba_kernel(slopes_ref, q_ref, k_ref, v_ref, z_ref, o_ref, ka_ref, va_ref):
    slope = slopes_ref[pl.program_id(0)]
    s_len = q_ref.shape[0]
    blk = MOBA_BLOCK
    nblk = s_len // blk

    @pl.when(pl.program_id(1) == 0)
    def _():
        pos = lax.broadcasted_iota(jnp.int32, (s_len, HEAD_DIM), 0)
        lane = lax.broadcasted_iota(jnp.int32, (s_len, HEAD_DIM), 1)
        bias = pos.astype(F32) * (slope / SM_SCALE)
        aug = jnp.where(lane == pos // blk, 1.0, 0.0)
        for part in range(N_BIAS_PARTS):
            piece = bias.astype(BF16).astype(F32)
            aug = jnp.where(lane == nblk + part, piece, aug)
            bias = bias - piece
        ka_ref[:, HEAD_DIM:] = aug.astype(BF16)
        va_ref[:, HEAD_DIM:] = jnp.ones((s_len, HEAD_DIM), BF16)

    ka_ref[:, :HEAD_DIM] = k_ref[...]
    va_ref[:, :HEAD_DIM] = v_ref[...]

    row = lax.broadcasted_iota(jnp.int32, (HEAD_DIM, s_len), 0)
    col = lax.broadcasted_iota(jnp.int32, (HEAD_DIM, s_len), 1)
    ind = jnp.where(col // blk == row, 1.0 / blk, 0.0).astype(BF16)
    kmean = jnp.dot(ind, k_ref[...], preferred_element_type=F32)
    k_hi = kmean.astype(BF16)
    k_lo = (kmean - k_hi.astype(F32)).astype(BF16)
    q = q_ref[...]
    gate_t = (_nt_dot(k_hi, q) + _nt_dot(k_lo, q))[:nblk, :]

    sub = lax.broadcasted_iota(jnp.int32, (nblk, blk), 0)
    tq = lax.broadcasted_iota(jnp.int32, (blk, blk), 0)
    sk = lax.broadcasted_iota(jnp.int32, (blk, blk), 1)
    causal = jnp.where(tq >= sk, 0.0, NEG)
    aug_row = lax.broadcasted_iota(jnp.int32, (HEAD_DIM, blk), 0)
    unit_rows = jnp.where((aug_row >= nblk) & (aug_row < nblk + N_BIAS_PARTS), 1.0, 0.0)
    pad_rows = jnp.zeros((HEAD_DIM - nblk, blk), F32)

    for n in range(nblk):
        rows = pl.ds(n * blk, blk)
        g = gate_t[:, n * blk:(n + 1) * blk]
        rank = jnp.zeros((nblk, blk), jnp.int32)
        for mp in range(n):
            g_mp = g[mp:mp + 1, :]
            beats = (g_mp > g) | ((g_mp == g) & (mp < sub))
            rank = rank + jnp.where(beats, 1, 0)
        allowed = ((sub < n) & (rank < MOBA_TOPK)) | (sub == n)
        mask_t = jnp.concatenate([jnp.where(allowed, 0.0, NEG), pad_rows], axis=0) + unit_rows
        qa = jnp.concatenate([q_ref[rows, :], mask_t.T.astype(BF16)], axis=1)

        nk = (n + 1) * blk
        d = _nt_dot(qa, ka_ref[pl.ds(0, nk), :])
        d_own = d[:, n * blk:] + causal
        m = jnp.max(d_own, axis=-1, keepdims=True)
        if n > 0:
            d_past = d[:, :n * blk]
            m = jnp.maximum(m, jnp.max(d_past, axis=-1, keepdims=True))
            p = jnp.concatenate([_exp_weights(d_past, m), _exp_weights(d_own, m)], axis=1)
        else:
            p = _exp_weights(d_own, m)
        acc = jnp.dot(p.astype(BF16), va_ref[pl.ds(0, nk), :], preferred_element_type=F32)
        o = acc[:, :HEAD_DIM] / acc[:, HEAD_DIM:]
        o_ref[rows, :] = (o * _silu(z_ref[rows, :].astype(F32))).astype(o_ref.dtype)


def _moba_attention(proj, slopes):
    b, s, _ = proj.shape
    blk = (None, s, HEAD_DIM)

    def col(c):
        return pl.BlockSpec(blk, lambda hi, bi, sl: (bi, 0, c * N_HEADS + hi))

    grid_spec = pltpu.PrefetchScalarGridSpec(
        num_scalar_prefetch=1,
        grid=(N_HEADS, b),
        in_specs=[col(c) for c in range(4)],
        out_specs=pl.BlockSpec(blk, lambda hi, bi, sl: (bi, 0, hi)),
        scratch_shapes=[pltpu.VMEM((s, 2 * HEAD_DIM), BF16), pltpu.VMEM((s, 2 * HEAD_DIM), BF16)],
    )
    return pl.pallas_call(
        _moba_kernel,
        grid_spec=grid_spec,
        out_shape=jax.ShapeDtypeStruct((b, s, N_HEADS * HEAD_DIM), BF16),
        compiler_params=_params("arbitrary", "arbitrary"),
        name="moba_mixer",
    )(slopes, proj, proj, proj, proj)


def kernel(x, c, norm_w, mod_w, mod_b, a_w_in, a_w_out, b_w_in, b_w_out, final_norm_w):
    b, s, d = x.shape
    depth = norm_w.shape[0]
    slopes = jnp.exp2(-8.0 * jnp.arange(1, N_HEADS + 1, dtype=F32) / N_HEADS)
    mod = _modulation(c, mod_w, mod_b)
    norm_w3 = norm_w.reshape(depth, 1, d)
    xf = x.reshape(b * s, d)
    for i in range(depth):
        j = i // 2
        final_w = final_norm_w if i == depth - 1 else None
        if i % 2 == 0:
            proj = _in_proj(_modulate(xf, norm_w3, mod, i, s, DILATIONS), a_w_in, i, j)
            og = _dilated_attention(proj.reshape(b, s, -1), slopes, PROJ_TM)
            xf = _out_proj(og.reshape(b * s, -1), a_w_out, xf, mod, i, j, s, final_w)
        else:
            proj = _in_proj(_modulate(xf, norm_w3, mod, i, s, (1,)), b_w_in, i, j)
            og = _moba_attention(proj.reshape(b, s, -1), slopes)
            xf = _out_proj(og.reshape(b * s, -1), b_w_out, xf, mod, i, j, s, final_w)
    return xf.reshape(b, s, d)
```

```python
import math

import jax
import jax.numpy as jnp
from jax import lax
from jax.experimental import pallas as pl
from jax.experimental.pallas import tpu as pltpu

F32 = jnp.float32
BF16 = jnp.bfloat16

HEAD_DIM = 128
N_HEADS = 16
DSWA_PATTERNS = ((128, 1), (512, 4), (2048, 16))
DILATIONS = tuple(d for _, d in DSWA_PATTERNS)
N_GROUPS = len(DSWA_PATTERNS)
QB = 128
MOBA_BLOCK = 256
MOBA_TOPK = 3
EPS = 1e-6
NEG = -1e30
SM_SCALE = HEAD_DIM ** -0.5
EXP2_SCALE = SM_SCALE * math.log2(math.e)

LANES = 128
VMEM_LIMIT_BYTES = 56 * 1024 * 1024
MOD_PAD_ROWS = 16
PROJ_TM = 1024
PROJ_TN = 1024


def _params(*sem):
    return pltpu.CompilerParams(dimension_semantics=sem, vmem_limit_bytes=VMEM_LIMIT_BYTES)


def _silu(z):
    return z / (1.0 + jnp.exp(-z))


def _mod_kernel(c_ref, w_ref, b_ref, o_ref):
    cond = _silu(c_ref[...])
    acc = jnp.dot(cond.astype(BF16), w_ref[...].astype(BF16), preferred_element_type=F32)
    o_ref[...] = acc + b_ref[...]


def _modulation(c, mod_w, mod_b):
    depth, d, n = mod_w.shape
    b = c.shape[0]
    tn = n // 4
    c_pad = jnp.pad(c, ((0, MOD_PAD_ROWS - b), (0, 0)))
    out = pl.pallas_call(
        _mod_kernel,
        grid=(depth, n // tn),
        in_specs=[
            pl.BlockSpec((MOD_PAD_ROWS, d), lambda l, j: (0, 0)),
            pl.BlockSpec((None, d, tn), lambda l, j: (l, 0, j)),
            pl.BlockSpec((None, 1, tn), lambda l, j: (l, 0, j)),
        ],
        out_specs=pl.BlockSpec((None, MOD_PAD_ROWS, tn), lambda l, j: (l, 0, j)),
        out_shape=jax.ShapeDtypeStruct((depth, MOD_PAD_ROWS, n), F32),
        compiler_params=_params("parallel", "arbitrary"),
        name="adaln_modulation",
    )(c_pad, mod_w, mod_b.reshape(depth, 1, n))
    return out[:, :b].reshape(depth, b, 3, d)


def _modulate_kernel(x_ref, nw_ref, mod_ref, h_ref, *stage, dilations):
    tm, d_model = x_ref.shape
    x = x_ref[...]
    rs = lax.rsqrt(jnp.mean(x * x, axis=-1, keepdims=True) + EPS)
    for c in range(d_model // LANES):
        cs = slice(c * LANES, (c + 1) * LANES)
        y = x_ref[:, cs] * rs * nw_ref[:, cs]
        hc = y * (1.0 + mod_ref[1:2, cs]) + mod_ref[0:1, cs]
        h_ref[0, :, cs] = hc.astype(BF16)
        if len(dilations) > 1:
            st = stage[0].at[c % 2]
            st[...] = hc
            for g, dil in enumerate(dilations[1:], 1):
                rows = tm // dil
                for r in range(dil):
                    h_ref[g, r * rows:(r + 1) * rows, cs] = (
                        st[pl.ds(r, rows, stride=dil), :].astype(BF16))


def _modulate(xf, norm_w3, mod, layer, seq, dilations):
    m, d = xf.shape
    tm = PROJ_TM
    tiles_per_batch = seq // tm
    n_copies = len(dilations)
    scratch = [pltpu.VMEM((2, tm, LANES), F32)] if n_copies > 1 else []
    kern = lambda *refs: _modulate_kernel(*refs, dilations=dilations)
    return pl.pallas_call(
        kern,
        grid=(m // tm,),
        in_specs=[
            pl.BlockSpec((tm, d), lambda i: (i, 0)),
            pl.BlockSpec((None, 1, d), lambda i: (layer, 0, 0)),
            pl.BlockSpec((None, None, 3, d), lambda i: (layer, i // tiles_per_batch, 0, 0)),
        ],
        out_specs=pl.BlockSpec((n_copies, tm, d), lambda i: (0, i, 0)),
        out_shape=jax.ShapeDtypeStruct((n_copies, m, d), BF16),
        scratch_shapes=scratch,
        compiler_params=_params("parallel"),
        name=f"modulate_l{layer}",
    )(xf, norm_w3, mod)


def _inproj_kernel(h_ref, w_ref, o_ref, wb_ref):
    @pl.when(pl.program_id(1) == 0)
    def _():
        wb_ref[...] = w_ref[...].astype(BF16)

    acc = jnp.dot(h_ref[...], wb_ref[...], preferred_element_type=F32)
    o_ref[...] = acc.astype(o_ref.dtype)


def _in_proj(h, w_in, layer, widx):
    n_copies, m, d = h.shape
    n = w_in.shape[-1]
    tm, tn = PROJ_TM, PROJ_TN
    tiles_per_copy = 3 * N_HEADS * HEAD_DIM // tn

    def copy_of(j):
        if n_copies == 1:
            return 0
        return jnp.where(j < tiles_per_copy * n_copies, j // tiles_per_copy, 0)

    return pl.pallas_call(
        _inproj_kernel,
        grid=(n // tn, m // tm),
        in_specs=[
            pl.BlockSpec((None, tm, d), lambda j, i: (copy_of(j), i, 0)),
            pl.BlockSpec((None, d, tn), lambda j, i: (widx, 0, j)),
        ],
        out_specs=pl.BlockSpec((tm, tn), lambda j, i: (i, j)),
        out_shape=jax.ShapeDtypeStruct((m, n), BF16),
        scratch_shapes=[pltpu.VMEM((d, tn), BF16)],
        compiler_params=_params("arbitrary", "arbitrary"),
        name=f"in_proj_l{layer}",
    )(h, w_in)


def _outproj_kernel(og_ref, w_ref, x_ref, mod_ref, *rest, final):
    if final:
        fw_ref, o_ref, wb_ref = rest
    else:
        o_ref, wb_ref = rest

    @pl.when(pl.program_id(1) == 0)
    def _():
        wb_ref[...] = w_ref[...].astype(BF16)

    y = jnp.dot(og_ref[...], wb_ref[...], preferred_element_type=F32)
    xn = x_ref[...] + mod_ref[2:3, :] * y
    if final:
        ms = jnp.mean(xn * xn, axis=-1, keepdims=True)
        xn = xn * lax.rsqrt(ms + EPS) * fw_ref[...]
    o_ref[...] = xn


def _out_proj(og, w_out, xf, mod, layer, widx, seq, final_w=None):
    m, k = og.shape
    n = w_out.shape[-1]
    final = final_w is not None
    tm, tn = (PROJ_TM // 2, n) if final else (PROJ_TM, PROJ_TN)
    tiles_per_batch = seq // tm
    w_mode = dict(pipeline_mode=pl.Buffered(1)) if final else {}
    in_specs = [
        pl.BlockSpec((tm, k), lambda j, i: (i, 0)),
        pl.BlockSpec((None, k, tn), lambda j, i: (widx, 0, j), **w_mode),
        pl.BlockSpec((tm, tn), lambda j, i: (i, j)),
        pl.BlockSpec((None, None, 3, tn), lambda j, i: (layer, i // tiles_per_batch, 0, j)),
    ]
    args = [og, w_out, xf, mod]
    if final:
        in_specs.append(pl.BlockSpec((1, n), lambda j, i: (0, 0)))
        args.append(final_w.reshape(1, n))
    kern = lambda *refs: _outproj_kernel(*refs, final=final)
    return pl.pallas_call(
        kern,
        grid=(n // tn, m // tm),
        in_specs=in_specs,
        out_specs=pl.BlockSpec((tm, tn), lambda j, i: (i, j)),
        out_shape=jax.ShapeDtypeStruct((m, n), F32),
        scratch_shapes=[pltpu.VMEM((k, tn), BF16)],
        compiler_params=_params("arbitrary", "arbitrary"),
        name=f"out_proj_l{layer}",
    )(*args)


def _nt_dot(a, b):
    return lax.dot_general(a, b, (((1,), (1,)), ((), ())), preferred_element_type=F32)


def _exp_weights(d, m):
    return jnp.exp2((d - m) * EXP2_SCALE)


def _pv_with_denominator(p, v):
    v1 = jnp.concatenate([v, jnp.ones(v.shape, v.dtype)], axis=1)
    acc = jnp.dot(p.astype(BF16), v1, preferred_element_type=F32)
    return acc[:, :HEAD_DIM], acc[:, HEAD_DIM:]


def _band_bias(slope, dilation, with_prev):
    nk = 2 * QB if with_prev else QB
    qi = lax.broadcasted_iota(jnp.int32, (QB, nk), 0)
    ki = lax.broadcasted_iota(jnp.int32, (QB, nk), 1)
    step = qi - ki + (QB if with_prev else 0)
    ok = (step >= 0) & (step <= QB)
    return jnp.where(ok, (-slope * dilation / SM_SCALE) * step.astype(F32), NEG)


def _softmax_parts(d, v):
    m = jnp.max(d, axis=-1, keepdims=True)
    acc, l = _pv_with_denominator(_exp_weights(d, m), v)
    return acc, l, jnp.broadcast_to(m, acc.shape)


def _dilated_kernel(slopes_ref, q0, k0, v0, q1, k1, v1, q2, k2, v2, z_ref, o_ref,
                    acc1, den1, max1, acc2, den2, max2, tacc2, tden2, tmax2, *, tile_rows):
    slope = slopes_ref[pl.program_id(1)]
    s_len = q0.shape[0]
    d0, d1, d2 = DILATIONS

    def rows(ref, start, size=QB):
        return ref[pl.ds(start, size), :]

    def prev_cur(ref, start_prev, start_cur):
        if start_cur == start_prev + QB:
            return rows(ref, start_prev, 2 * QB)
        return jnp.concatenate([rows(ref, start_prev), rows(ref, start_cur)], axis=0)

    per_res1 = tile_rows // d1
    per_res2 = tile_rows // d2
    n_tiles = s_len // tile_rows
    nq1 = s_len // d1 // QB

    def start1(r, n):
        l0 = n * QB
        return (l0 // per_res1) * tile_rows + r * per_res1 + l0 % per_res1

    def gather2(ref, r):
        return jnp.concatenate([rows(ref, t * tile_rows + r * per_res2, per_res2)
                                for t in range(n_tiles)], axis=0)

    bias = {(d, wp): _band_bias(slope, d, wp) for d in DILATIONS for wp in (False, True)}
    scores1, scores2, scores0 = {}, {}, {}
    for r in range(d1):
        for n in range(nq1):
            k = rows(k1, start1(r, 0)) if n == 0 else prev_cur(k1, start1(r, n - 1), start1(r, n))
            scores1[r, n] = _nt_dot(rows(q1, start1(r, n)), k) + bias[d1, n > 0]
    for r in range(d2):
        scores2[r] = _nt_dot(gather2(q2, r), gather2(k2, r)) + bias[d2, False]
    for n in range(s_len // QB):
        k = rows(k0, 0) if n == 0 else rows(k0, (n - 1) * QB, 2 * QB)
        scores0[n] = _nt_dot(rows(q0, n * QB), k) + bias[d0, n > 0]

    for r in range(d1):
        for n in range(nq1):
            v = rows(v1, start1(r, 0)) if n == 0 else prev_cur(v1, start1(r, n - 1), start1(r, n))
            dst = pl.ds(n * QB * d1 + r, QB, stride=d1)
            for ref, val in zip((acc1, den1, max1), _softmax_parts(scores1[r, n], v)):
                ref[dst, :] = val

    for r in range(d2):
        dst = pl.ds((r % d1) * (s_len // d1) + r // d1, QB, stride=d2 // d1)
        for ref, val in zip((tacc2, tden2, tmax2), _softmax_parts(scores2[r], gather2(v2, r))):
            ref[dst, :] = val
    for r in range(d1):
        src = pl.ds(r * (s_len // d1), s_len // d1)
        dst = pl.ds(r, s_len // d1, stride=d1)
        for ref, tmp in ((acc2, tacc2), (den2, tden2), (max2, tmax2)):
            ref[dst, :] = tmp[src, :]

    for n in range(s_len // QB):
        v = rows(v0, 0) if n == 0 else rows(v0, (n - 1) * QB, 2 * QB)
        a0, l0, m0 = _softmax_parts(scores0[n], v)
        blk = pl.ds(n * QB, QB)
        m1 = max1[blk, :]
        m2 = max2[blk, :]
        mx = jnp.maximum(jnp.maximum(m0, m1), m2)
        w0 = _exp_weights(m0, mx)
        w1 = _exp_weights(m1, mx)
        w2 = _exp_weights(m2, mx)
        num = w0 * a0 + w1 * acc1[blk, :] + w2 * acc2[blk, :]
        den = w0 * l0 + w1 * den1[blk, :] + w2 * den2[blk, :]
        o_ref[blk, :] = (num / den * _silu(z_ref[blk, :].astype(F32))).astype(o_ref.dtype)


def _dilated_attention(proj, slopes, tile_rows):
    b, s, _ = proj.shape
    blk = (None, s, HEAD_DIM)

    def col(c):
        return pl.BlockSpec(blk, lambda bi, hi, sl: (bi, 0, c * N_HEADS + hi))

    grid_spec = pltpu.PrefetchScalarGridSpec(
        num_scalar_prefetch=1,
        grid=(b, N_HEADS),
        in_specs=[col(c) for c in range(3 * N_GROUPS + 1)],
        out_specs=pl.BlockSpec(blk, lambda bi, hi, sl: (bi, 0, hi)),
        scratch_shapes=[pltpu.VMEM((s, HEAD_DIM), F32) for _ in range(9)],
    )
    kern = lambda *refs: _dilated_kernel(*refs, tile_rows=tile_rows)
    return pl.pallas_call(
        kern,
        grid_spec=grid_spec,
        out_shape=jax.ShapeDtypeStruct((b, s, N_HEADS * HEAD_DIM), BF16),
        compiler_params=_params("parallel", "arbitrary"),
        name="dilated_mixer",
    )(slopes, *([proj] * (3 * N_GROUPS + 1)))


N_BIAS_PARTS = 3
GATE_ROWS = 16


def _moba_kernel(slopes_ref, q_ref, k_ref, v_ref, z_ref, o_ref, ka_ref, va_ref):
    slope = slopes_ref[pl.program_id(0)]
    s_len = q_ref.shape[0]
    blk = MOBA_BLOCK
    nblk = s_len // blk

    @pl.when(pl.program_id(1) == 0)
    def _():
        pos = lax.broadcasted_iota(jnp.int32, (s_len, HEAD_DIM), 0)
        lane = lax.broadcasted_iota(jnp.int32, (s_len, HEAD_DIM), 1)
        bias = pos.astype(F32) * (slope / SM_SCALE)
        aug = jnp.where(lane == pos // blk, 1.0, 0.0)
        for part in range(N_BIAS_PARTS):
            piece = bias.astype(BF16).astype(F32)
            aug = jnp.where(lane == nblk + part, piece, aug)
            bias = bias - piece
        ka_ref[:, HEAD_DIM:] = aug.astype(BF16)
        va_ref[:, HEAD_DIM:] = jnp.ones((s_len, HEAD_DIM), BF16)

    ka_ref[:, :HEAD_DIM] = k_ref[...]
    va_ref[:, :HEAD_DIM] = v_ref[...]

    row = lax.broadcasted_iota(jnp.int32, (GATE_ROWS, s_len), 0)
    col = lax.broadcasted_iota(jnp.int32, (GATE_ROWS, s_len), 1)
    ind = jnp.where(col // blk == row, 1.0 / blk, 0.0).astype(BF16)
    kmean = jnp.dot(ind, k_ref[...], preferred_element_type=F32)
    k_hi = kmean.astype(BF16)
    k_lo = (kmean - k_hi.astype(F32)).astype(BF16)
    q = q_ref[...]
    gate_t = (_nt_dot(k_hi, q) + _nt_dot(k_lo, q))[:nblk, :]

    sub = lax.broadcasted_iota(jnp.int32, (nblk, blk), 0)
    tq = lax.broadcasted_iota(jnp.int32, (blk, blk), 0)
    sk = lax.broadcasted_iota(jnp.int32, (blk, blk), 1)
    causal = jnp.where(tq >= sk, 0.0, NEG)
    aug_row = lax.broadcasted_iota(jnp.int32, (HEAD_DIM, blk), 0)
    unit_rows = jnp.where((aug_row >= nblk) & (aug_row < nblk + N_BIAS_PARTS), 1.0, 0.0)
    pad_rows = jnp.zeros((HEAD_DIM - nblk, blk), F32)

    scores = []
    for n in range(nblk):
        rows = pl.ds(n * blk, blk)
        g = gate_t[:, n * blk:(n + 1) * blk]
        rank = jnp.zeros((nblk, blk), jnp.int32)
        for mp in range(n):
            g_mp = g[mp:mp + 1, :]
            beats = (g_mp > g) | ((g_mp == g) & (mp < sub))
            rank = rank + jnp.where(beats, 1, 0)
        allowed = ((sub < n) & (rank < MOBA_TOPK)) | (sub == n)
        mask_t = jnp.concatenate([jnp.where(allowed, 0.0, NEG), pad_rows], axis=0) + unit_rows
        qa = jnp.concatenate([q_ref[rows, :], mask_t.T.astype(BF16)], axis=1)
        scores.append(_nt_dot(qa, ka_ref[pl.ds(0, (n + 1) * blk), :]))

    for n in range(nblk):
        rows = pl.ds(n * blk, blk)
        nk = (n + 1) * blk
        d = scores[n]
        d_own = d[:, n * blk:] + causal
        m = jnp.max(d_own, axis=-1, keepdims=True)
        if n > 0:
            d_past = d[:, :n * blk]
            m = jnp.maximum(m, jnp.max(d_past, axis=-1, keepdims=True))
            p = jnp.concatenate([_exp_weights(d_past, m), _exp_weights(d_own, m)], axis=1)
        else:
            p = _exp_weights(d_own, m)
        acc = jnp.dot(p.astype(BF16), va_ref[pl.ds(0, nk), :], preferred_element_type=F32)
        o = acc[:, :HEAD_DIM] / acc[:, HEAD_DIM:]
        o_ref[rows, :] = (o * _silu(z_ref[rows, :].astype(F32))).astype(o_ref.dtype)


def _moba_attention(proj, slopes):
    b, s, _ = proj.shape
    blk = (None, s, HEAD_DIM)

    def col(c):
        return pl.BlockSpec(blk, lambda hi, bi, sl: (bi, 0, c * N_HEADS + hi))

    grid_spec = pltpu.PrefetchScalarGridSpec(
        num_scalar_prefetch=1,
        grid=(N_HEADS, b),
        in_specs=[col(c) for c in range(4)],
        out_specs=pl.BlockSpec(blk, lambda hi, bi, sl: (bi, 0, hi)),
        scratch_shapes=[pltpu.VMEM((s, 2 * HEAD_DIM), BF16), pltpu.VMEM((s, 2 * HEAD_DIM), BF16)],
    )
    return pl.pallas_call(
        _moba_kernel,
        grid_spec=grid_spec,
        out_shape=jax.ShapeDtypeStruct((b, s, N_HEADS * HEAD_DIM), BF16),
        compiler_params=_params("arbitrary", "arbitrary"),
        name="moba_mixer",
    )(slopes, proj, proj, proj, proj)


def kernel(x, c, norm_w, mod_w, mod_b, a_w_in, a_w_out, b_w_in, b_w_out, final_norm_w):
    b, s, d = x.shape
    depth = norm_w.shape[0]
    slopes = jnp.exp2(-8.0 * jnp.arange(1, N_HEADS + 1, dtype=F32) / N_HEADS)
    mod = _modulation(c, mod_w, mod_b)
    norm_w3 = norm_w.reshape(depth, 1, d)
    xf = x.reshape(b * s, d)
    for i in range(depth):
        j = i // 2
        final_w = final_norm_w if i == depth - 1 else None
        if i % 2 == 0:
            proj = _in_proj(_modulate(xf, norm_w3, mod, i, s, DILATIONS), a_w_in, i, j)
            og = _dilated_attention(proj.reshape(b, s, -1), slopes, PROJ_TM)
            xf = _out_proj(og.reshape(b * s, -1), a_w_out, xf, mod, i, j, s, final_w)
        else:
            proj = _in_proj(_modulate(xf, norm_w3, mod, i, s, (1,)), b_w_in, i, j)
            og = _moba_attention(proj.reshape(b, s, -1), slopes)
            xf = _out_proj(og.reshape(b * s, -1), b_w_out, xf, mod, i, j, s, final_w)
    return xf.reshape(b, s, d)
```

```python
import math

import jax
import jax.numpy as jnp
from jax import lax
from jax.experimental import pallas as pl
from jax.experimental.pallas import tpu as pltpu

F32 = jnp.float32
BF16 = jnp.bfloat16

HEAD_DIM = 128
N_HEADS = 16
DSWA_PATTERNS = ((128, 1), (512, 4), (2048, 16))
DILATIONS = tuple(d for _, d in DSWA_PATTERNS)
N_GROUPS = len(DSWA_PATTERNS)
QB = 128
MOBA_BLOCK = 256
MOBA_TOPK = 3
EPS = 1e-6
NEG = -1e30
SM_SCALE = HEAD_DIM ** -0.5
EXP2_SCALE = SM_SCALE * math.log2(math.e)

LANES = 128
VMEM_LIMIT_BYTES = 56 * 1024 * 1024
MOD_PAD_ROWS = 16
PROJ_TM = 1024
PROJ_TN = 1024
SCORE_LOOKAHEAD = 4


def _params(*sem):
    return pltpu.CompilerParams(dimension_semantics=sem, vmem_limit_bytes=VMEM_LIMIT_BYTES)


def _silu(z):
    return z / (1.0 + jnp.exp(-z))


def _mod_kernel(c_ref, w_ref, b_ref, o_ref):
    cond = _silu(c_ref[...])
    acc = jnp.dot(cond.astype(BF16), w_ref[...].astype(BF16), preferred_element_type=F32)
    o_ref[...] = acc + b_ref[...]


def _modulation(c, mod_w, mod_b):
    depth, d, n = mod_w.shape
    b = c.shape[0]
    tn = n // 4
    c_pad = jnp.pad(c, ((0, MOD_PAD_ROWS - b), (0, 0)))
    out = pl.pallas_call(
        _mod_kernel,
        grid=(depth, n // tn),
        in_specs=[
            pl.BlockSpec((MOD_PAD_ROWS, d), lambda l, j: (0, 0)),
            pl.BlockSpec((None, d, tn), lambda l, j: (l, 0, j)),
            pl.BlockSpec((None, 1, tn), lambda l, j: (l, 0, j)),
        ],
        out_specs=pl.BlockSpec((None, MOD_PAD_ROWS, tn), lambda l, j: (l, 0, j)),
        out_shape=jax.ShapeDtypeStruct((depth, MOD_PAD_ROWS, n), F32),
        compiler_params=_params("parallel", "arbitrary"),
        name="adaln_modulation",
    )(c_pad, mod_w, mod_b.reshape(depth, 1, n))
    return out[:, :b].reshape(depth, b, 3, d)


def _modulate_kernel(x_ref, nw_ref, mod_ref, h_ref, *stage, dilations):
    tm, d_model = x_ref.shape
    x = x_ref[...]
    rs = lax.rsqrt(jnp.mean(x * x, axis=-1, keepdims=True) + EPS)
    for c in range(d_model // LANES):
        cs = slice(c * LANES, (c + 1) * LANES)
        y = x_ref[:, cs] * rs * nw_ref[:, cs]
        hc = y * (1.0 + mod_ref[1:2, cs]) + mod_ref[0:1, cs]
        h_ref[0, :, cs] = hc.astype(BF16)
        n_levels = len(dilations) - 1
        level = hc
        for g in range(1, n_levels + 1):
            src = stage[0].at[(g - 1) * 2 + c % 2]
            src[...] = level
            prev, dil = dilations[g - 1], dilations[g]
            ratio, rows = dil // prev, tm // dil
            slabs = [None] * dil
            for rp in range(prev):
                for q in range(ratio):
                    slabs[rp + prev * q] = src[pl.ds(rp * (tm // prev) + q, rows, stride=ratio), :]
            level = jnp.concatenate(slabs, axis=0)
            h_ref[g, :, cs] = level.astype(BF16)


def _modulate(xf, norm_w3, mod, layer, seq, dilations):
    m, d = xf.shape
    n_copies = len(dilations)
    tm = PROJ_TM if n_copies > 1 else PROJ_TM // 2
    tiles_per_batch = seq // tm
    scratch = [pltpu.VMEM((2 * (n_copies - 1), tm, LANES), F32)] if n_copies > 1 else []
    kern = lambda *refs: _modulate_kernel(*refs, dilations=dilations)
    return pl.pallas_call(
        kern,
        grid=(m // tm,),
        in_specs=[
            pl.BlockSpec((tm, d), lambda i: (i, 0)),
            pl.BlockSpec((None, 1, d), lambda i: (layer, 0, 0)),
            pl.BlockSpec((None, None, 3, d), lambda i: (layer, i // tiles_per_batch, 0, 0)),
        ],
        out_specs=pl.BlockSpec((n_copies, tm, d), lambda i: (0, i, 0)),
        out_shape=jax.ShapeDtypeStruct((n_copies, m, d), BF16),
        scratch_shapes=scratch,
        compiler_params=_params("parallel"),
        name=f"modulate_l{layer}",
    )(xf, norm_w3, mod)


def _inproj_kernel(h_ref, w_ref, o_ref, wb_ref):
    @pl.when(pl.program_id(1) == 0)
    def _():
        wb_ref[...] = w_ref[...].astype(BF16)

    acc = jnp.dot(h_ref[...], wb_ref[...], preferred_element_type=F32)
    o_ref[...] = acc.astype(o_ref.dtype)


def _in_proj(h, w_in, layer, widx):
    n_copies, m, d = h.shape
    n = w_in.shape[-1]
    tm, tn = PROJ_TM, PROJ_TN
    tiles_per_copy = 3 * N_HEADS * HEAD_DIM // tn

    def copy_of(j):
        if n_copies == 1:
            return 0
        return jnp.where(j < tiles_per_copy * n_copies, j // tiles_per_copy, 0)

    return pl.pallas_call(
        _inproj_kernel,
        grid=(n // tn, m // tm),
        in_specs=[
            pl.BlockSpec((None, tm, d), lambda j, i: (copy_of(j), i, 0)),
            pl.BlockSpec((None, d, tn), lambda j, i: (widx, 0, j)),
        ],
        out_specs=pl.BlockSpec((tm, tn), lambda j, i: (i, j)),
        out_shape=jax.ShapeDtypeStruct((m, n), BF16),
        scratch_shapes=[pltpu.VMEM((d, tn), BF16)],
        compiler_params=_params("arbitrary", "arbitrary"),
        name=f"in_proj_l{layer}",
    )(h, w_in)


def _outproj_kernel(og_ref, w_ref, x_ref, mod_ref, *rest, final):
    if final:
        fw_ref, o_ref, wb_ref = rest
    else:
        o_ref, wb_ref = rest

    @pl.when(pl.program_id(1) == 0)
    def _():
        wb_ref[...] = w_ref[...].astype(BF16)

    y = jnp.dot(og_ref[...], wb_ref[...], preferred_element_type=F32)
    xn = x_ref[...] + mod_ref[2:3, :] * y
    if final:
        ms = jnp.mean(xn * xn, axis=-1, keepdims=True)
        xn = xn * lax.rsqrt(ms + EPS) * fw_ref[...]
    o_ref[...] = xn


def _out_proj(og, w_out, xf, mod, layer, widx, seq, final_w=None):
    m, k = og.shape
    n = w_out.shape[-1]
    final = final_w is not None
    tm, tn = (PROJ_TM // 2, n) if final else (PROJ_TM, PROJ_TN)
    tiles_per_batch = seq // tm
    w_mode = dict(pipeline_mode=pl.Buffered(1)) if final else {}
    in_specs = [
        pl.BlockSpec((tm, k), lambda j, i: (i, 0)),
        pl.BlockSpec((None, k, tn), lambda j, i: (widx, 0, j), **w_mode),
        pl.BlockSpec((tm, tn), lambda j, i: (i, j)),
        pl.BlockSpec((None, None, 3, tn), lambda j, i: (layer, i // tiles_per_batch, 0, j)),
    ]
    args = [og, w_out, xf, mod]
    if final:
        in_specs.append(pl.BlockSpec((1, n), lambda j, i: (0, 0)))
        args.append(final_w.reshape(1, n))
    kern = lambda *refs: _outproj_kernel(*refs, final=final)
    return pl.pallas_call(
        kern,
        grid=(n // tn, m // tm),
        in_specs=in_specs,
        out_specs=pl.BlockSpec((tm, tn), lambda j, i: (i, j)),
        out_shape=jax.ShapeDtypeStruct((m, n), F32),
        scratch_shapes=[pltpu.VMEM((k, tn), BF16)],
        compiler_params=_params("arbitrary", "arbitrary"),
        name=f"out_proj_l{layer}",
    )(*args)


def _nt_dot(a, b):
    return lax.dot_general(a, b, (((1,), (1,)), ((), ())), preferred_element_type=F32)


def _exp_weights(d, m):
    return jnp.exp2((d - m) * EXP2_SCALE)


def _pv_with_denominator(p, v):
    v1 = jnp.concatenate([v, jnp.ones(v.shape, v.dtype)], axis=1)
    acc = jnp.dot(p.astype(BF16), v1, preferred_element_type=F32)
    return acc[:, :HEAD_DIM], acc[:, HEAD_DIM:]


def _band_bias(slope, dilation, with_prev):
    nk = 2 * QB if with_prev else QB
    qi = lax.broadcasted_iota(jnp.int32, (QB, nk), 0)
    ki = lax.broadcasted_iota(jnp.int32, (QB, nk), 1)
    step = qi - ki + (QB if with_prev else 0)
    ok = (step >= 0) & (step <= QB)
    return jnp.where(ok, (-slope * dilation / SM_SCALE) * step.astype(F32), NEG)


def _softmax_parts(d, v):
    m = jnp.max(d, axis=-1, keepdims=True)
    acc, l = _pv_with_denominator(_exp_weights(d, m), v)
    return acc, l, jnp.broadcast_to(m, acc.shape)


def _softmax_out_lse(d, v):
    acc, l, m = _softmax_parts(d, v)
    return acc / l, m * SM_SCALE + jnp.log(l)


def _dilated_kernel(slopes_ref, q0, k0, v0, q1, k1, v1, q2, k2, v2, z_ref, o_ref,
                    out1, lse1, out2, lse2, tout2, tlse2, *, tile_rows):
    slope = slopes_ref[pl.program_id(1)]
    s_len = q0.shape[0]
    d0, d1, d2 = DILATIONS

    def rows(ref, start, size=QB):
        return ref[pl.ds(start, size), :]

    def prev_cur(ref, start_prev, start_cur):
        if start_cur == start_prev + QB:
            return rows(ref, start_prev, 2 * QB)
        return jnp.concatenate([rows(ref, start_prev), rows(ref, start_cur)], axis=0)

    per_res1 = tile_rows // d1
    per_res2 = tile_rows // d2
    n_tiles = s_len // tile_rows
    nq1 = s_len // d1 // QB

    def start1(r, n):
        l0 = n * QB
        return (l0 // per_res1) * tile_rows + r * per_res1 + l0 % per_res1

    def gather2(ref, r):
        return jnp.concatenate([rows(ref, t * tile_rows + r * per_res2, per_res2)
                                for t in range(n_tiles)], axis=0)

    bias = {(d, wp): _band_bias(slope, d, wp) for d in DILATIONS for wp in (False, True)}
    tasks = []

    def task1(r, n):
        def kv(ref):
            return rows(ref, start1(r, 0)) if n == 0 else prev_cur(ref, start1(r, n - 1), start1(r, n))

        def finish(d):
            dst = pl.ds(n * QB * d1 + r, QB, stride=d1)
            out1[dst, :], lse1[dst, :] = _softmax_out_lse(d, kv(v1))

        return (lambda: _nt_dot(rows(q1, start1(r, n)), kv(k1)) + bias[d1, n > 0]), finish

    tasks += [task1(r, n) for r in range(d1) for n in range(nq1)]

    def task2(r):
        def finish(d):
            dst = pl.ds((r % d1) * (s_len // d1) + r // d1, QB, stride=d2 // d1)
            tout2[dst, :], tlse2[dst, :] = _softmax_out_lse(d, gather2(v2, r))

        return (lambda: _nt_dot(gather2(q2, r), gather2(k2, r)) + bias[d2, False]), finish

    def second_hop(_):
        for r in range(d1):
            src = pl.ds(r * (s_len // d1), s_len // d1)
            dst = pl.ds(r, s_len // d1, stride=d1)
            out2[dst, :] = tout2[src, :]
            lse2[dst, :] = tlse2[src, :]

    tasks += [task2(r) for r in range(d2)] + [((lambda: None), second_hop)]

    def task0(n):
        def kv(ref):
            return rows(ref, 0) if n == 0 else rows(ref, (n - 1) * QB, 2 * QB)

        def finish(d):
            a0, l0, m0 = _softmax_parts(d, kv(v0))
            blk = pl.ds(n * QB, QB)
            s0 = m0 * SM_SCALE
            s1 = lse1[blk, :]
            s2 = lse2[blk, :]
            mx = jnp.maximum(jnp.maximum(s0, s1), s2)
            w0 = jnp.exp(s0 - mx)
            w1 = jnp.exp(s1 - mx)
            w2 = jnp.exp(s2 - mx)
            num = w0 * a0 + w1 * out1[blk, :] + w2 * out2[blk, :]
            den = w0 * l0 + w1 + w2
            o_ref[blk, :] = (num / den * _silu(z_ref[blk, :].astype(F32))).astype(o_ref.dtype)

        return (lambda: _nt_dot(rows(q0, n * QB), kv(k0)) + bias[d0, n > 0]), finish

    tasks += [task0(n) for n in range(s_len // QB)]

    scores = [score() for score, _ in tasks[:SCORE_LOOKAHEAD]]
    for i, (_, finish) in enumerate(tasks):
        if i + SCORE_LOOKAHEAD < len(tasks):
            scores.append(tasks[i + SCORE_LOOKAHEAD][0]())
        finish(scores[i])


def _dilated_attention(proj, slopes, tile_rows):
    b, s, _ = proj.shape
    blk = (None, s, HEAD_DIM)

    def col(c):
        return pl.BlockSpec(blk, lambda bi, hi, sl: (bi, 0, c * N_HEADS + hi))

    grid_spec = pltpu.PrefetchScalarGridSpec(
        num_scalar_prefetch=1,
        grid=(b, N_HEADS),
        in_specs=[col(c) for c in range(3 * N_GROUPS + 1)],
        out_specs=pl.BlockSpec(blk, lambda bi, hi, sl: (bi, 0, hi)),
        scratch_shapes=[pltpu.VMEM((s, HEAD_DIM), F32) for _ in range(6)],
    )
    kern = lambda *refs: _dilated_kernel(*refs, tile_rows=tile_rows)
    return pl.pallas_call(
        kern,
        grid_spec=grid_spec,
        out_shape=jax.ShapeDtypeStruct((b, s, N_HEADS * HEAD_DIM), BF16),
        compiler_params=_params("parallel", "arbitrary"),
        name="dilated_mixer",
    )(slopes, *([proj] * (3 * N_GROUPS + 1)))


N_BIAS_PARTS = 3
GATE_ROWS = 16


def _moba_kernel(slopes_ref, q_ref, k_ref, v_ref, z_ref, o_ref, ka_ref, va_ref):
    slope = slopes_ref[pl.program_id(0)]
    s_len = q_ref.shape[0]
    blk = MOBA_BLOCK
    nblk = s_len // blk

    @pl.when(pl.program_id(1) == 0)
    def _():
        pos = lax.broadcasted_iota(jnp.int32, (s_len, HEAD_DIM), 0)
        lane = lax.broadcasted_iota(jnp.int32, (s_len, HEAD_DIM), 1)
        bias = pos.astype(F32) * (slope / SM_SCALE)
        aug = jnp.where(lane == pos // blk, 1.0, 0.0)
        for part in range(N_BIAS_PARTS):
            piece = bias.astype(BF16).astype(F32)
            aug = jnp.where(lane == nblk + part, piece, aug)
            bias = bias - piece
        ka_ref[:, HEAD_DIM:] = aug.astype(BF16)
        va_ref[:, HEAD_DIM:] = jnp.ones((s_len, HEAD_DIM), BF16)

    ka_ref[:, :HEAD_DIM] = k_ref[...]
    va_ref[:, :HEAD_DIM] = v_ref[...]

    row = lax.broadcasted_iota(jnp.int32, (GATE_ROWS, s_len), 0)
    col = lax.broadcasted_iota(jnp.int32, (GATE_ROWS, s_len), 1)
    ind = jnp.where(col // blk == row, 1.0 / blk, 0.0).astype(BF16)
    kmean = jnp.dot(ind, k_ref[...], preferred_element_type=F32)
    k_hi = kmean.astype(BF16)
    k_lo = (kmean - k_hi.astype(F32)).astype(BF16)
    q = q_ref[...]
    gate_t = (_nt_dot(k_hi, q) + _nt_dot(k_lo, q))[:nblk, :]

    sub = lax.broadcasted_iota(jnp.int32, (nblk, blk), 0)
    tq = lax.broadcasted_iota(jnp.int32, (blk, blk), 0)
    sk = lax.broadcasted_iota(jnp.int32, (blk, blk), 1)
    causal = jnp.where(tq >= sk, 0.0, NEG)
    aug_row = lax.broadcasted_iota(jnp.int32, (HEAD_DIM, blk), 0)
    unit_rows = jnp.where((aug_row >= nblk) & (aug_row < nblk + N_BIAS_PARTS), 1.0, 0.0)
    pad_rows = jnp.zeros((HEAD_DIM - nblk, blk), F32)

    scores = []
    for n in range(nblk):
        rows = pl.ds(n * blk, blk)
        g = gate_t[:, n * blk:(n + 1) * blk]
        rank = jnp.zeros((nblk, blk), jnp.int32)
        for mp in range(n):
            g_mp = g[mp:mp + 1, :]
            beats = (g_mp > g) | ((g_mp == g) & (mp < sub))
            rank = rank + jnp.where(beats, 1, 0)
        allowed = ((sub < n) & (rank < MOBA_TOPK)) | (sub == n)
        mask_t = jnp.concatenate([jnp.where(allowed, 0.0, NEG), pad_rows], axis=0) + unit_rows
        qa = jnp.concatenate([q_ref[rows, :], mask_t.T.astype(BF16)], axis=1)
        scores.append(_nt_dot(qa, ka_ref[pl.ds(0, (n + 1) * blk), :]))

    for n in range(nblk):
        rows = pl.ds(n * blk, blk)
        nk = (n + 1) * blk
        d = scores[n]
        d_own = d[:, n * blk:] + causal
        m = jnp.max(d_own, axis=-1, keepdims=True)
        if n > 0:
            d_past = d[:, :n * blk]
            m = jnp.maximum(m, jnp.max(d_past, axis=-1, keepdims=True))
            p = jnp.concatenate([_exp_weights(d_past, m), _exp_weights(d_own, m)], axis=1)
        else:
            p = _exp_weights(d_own, m)
        acc = jnp.dot(p.astype(BF16), va_ref[pl.ds(0, nk), :], preferred_element_type=F32)
        o = acc[:, :HEAD_DIM] / acc[:, HEAD_DIM:]
        o_ref[rows, :] = (o * _silu(z_ref[rows, :].astype(F32))).astype(o_ref.dtype)


def _moba_attention(proj, slopes):
    b, s, _ = proj.shape
    blk = (None, s, HEAD_DIM)

    def col(c):
        return pl.BlockSpec(blk, lambda hi, bi, sl: (bi, 0, c * N_HEADS + hi))

    grid_spec = pltpu.PrefetchScalarGridSpec(
        num_scalar_prefetch=1,
        grid=(N_HEADS, b),
        in_specs=[col(c) for c in range(4)],
        out_specs=pl.BlockSpec(blk, lambda hi, bi, sl: (bi, 0, hi)),
        scratch_shapes=[pltpu.VMEM((s, 2 * HEAD_DIM), BF16), pltpu.VMEM((s, 2 * HEAD_DIM), BF16)],
    )
    return pl.pallas_call(
        _moba_kernel,
        grid_spec=grid_spec,
        out_shape=jax.ShapeDtypeStruct((b, s, N_HEADS * HEAD_DIM), BF16),
        compiler_params=_params("arbitrary", "arbitrary"),
        name="moba_mixer",
    )(slopes, proj, proj, proj, proj)


def kernel(x, c, norm_w, mod_w, mod_b, a_w_in, a_w_out, b_w_in, b_w_out, final_norm_w):
    b, s, d = x.shape
    depth = norm_w.shape[0]
    slopes = jnp.exp2(-8.0 * jnp.arange(1, N_HEADS + 1, dtype=F32) / N_HEADS)
    mod = _modulation(c, mod_w, mod_b)
    norm_w3 = norm_w.reshape(depth, 1, d)
    xf = x.reshape(b * s, d)
    for i in range(depth):
        j = i // 2
        final_w = final_norm_w if i == depth - 1 else None
        if i % 2 == 0:
            proj = _in_proj(_modulate(xf, norm_w3, mod, i, s, DILATIONS), a_w_in, i, j)
            og = _dilated_attention(proj.reshape(b, s, -1), slopes, PROJ_TM)
            xf = _out_proj(og.reshape(b * s, -1), a_w_out, xf, mod, i, j, s, final_w)
        else:
            proj = _in_proj(_modulate(xf, norm_w3, mod, i, s, (1,)), b_w_in, i, j)
            og = _moba_attention(proj.reshape(b, s, -1), slopes)
            xf = _out_proj(og.reshape(b * s, -1), b_w_out, xf, mod, i, j, s, final_w)
    return xf.reshape(b, s, d)
```

```python
import math

import jax
import jax.numpy as jnp
from jax import lax
from jax.experimental import pallas as pl
from jax.experimental.pallas import tpu as pltpu

F32 = jnp.float32
BF16 = jnp.bfloat16

HEAD_DIM = 128
N_HEADS = 16
DSWA_PATTERNS = ((128, 1), (512, 4), (2048, 16))
DILATIONS = tuple(d for _, d in DSWA_PATTERNS)
N_GROUPS = len(DSWA_PATTERNS)
QB = 128
MOBA_BLOCK = 256
MOBA_TOPK = 3
EPS = 1e-6
NEG = -1e30
SM_SCALE = HEAD_DIM ** -0.5
EXP2_SCALE = SM_SCALE * math.log2(math.e)

LANES = 128
VMEM_LIMIT_BYTES = 56 * 1024 * 1024
MOD_PAD_ROWS = 16
PROJ_TM = 1024
PROJ_TN = 1024
SCORE_LOOKAHEAD = 4


def _params(*sem):
    return pltpu.CompilerParams(dimension_semantics=sem, vmem_limit_bytes=VMEM_LIMIT_BYTES)


def _silu(z):
    return z / (1.0 + jnp.exp(-z))


def _mod_kernel(c_ref, w_ref, b_ref, o_ref):
    cond = _silu(c_ref[...])
    acc = jnp.dot(cond.astype(BF16), w_ref[...].astype(BF16), preferred_element_type=F32)
    o_ref[...] = acc + b_ref[...]


def _modulation(c, mod_w, mod_b):
    depth, d, n = mod_w.shape
    b = c.shape[0]
    tn = n // 4
    c_pad = jnp.pad(c, ((0, MOD_PAD_ROWS - b), (0, 0)))
    out = pl.pallas_call(
        _mod_kernel,
        grid=(depth, n // tn),
        in_specs=[
            pl.BlockSpec((MOD_PAD_ROWS, d), lambda l, j: (0, 0)),
            pl.BlockSpec((None, d, tn), lambda l, j: (l, 0, j)),
            pl.BlockSpec((None, 1, tn), lambda l, j: (l, 0, j)),
        ],
        out_specs=pl.BlockSpec((None, MOD_PAD_ROWS, tn), lambda l, j: (l, 0, j)),
        out_shape=jax.ShapeDtypeStruct((depth, MOD_PAD_ROWS, n), F32),
        compiler_params=_params("parallel", "arbitrary"),
        name="adaln_modulation",
    )(c_pad, mod_w, mod_b.reshape(depth, 1, n))
    return out[:, :b].reshape(depth, b, 3, d)


def _modulate_kernel(x_ref, nw_ref, mod_ref, h_ref, *stage, dilations):
    tm, d_model = x_ref.shape
    x = x_ref[...]
    rs = lax.rsqrt(jnp.mean(x * x, axis=-1, keepdims=True) + EPS)
    for c in range(d_model // LANES):
        cs = slice(c * LANES, (c + 1) * LANES)
        y = x_ref[:, cs] * rs * nw_ref[:, cs]
        hc = y * (1.0 + mod_ref[1:2, cs]) + mod_ref[0:1, cs]
        h_ref[0, :, cs] = hc.astype(BF16)
        n_levels = len(dilations) - 1
        level = hc
        for g in range(1, n_levels + 1):
            src = stage[0].at[(g - 1) * 2 + c % 2]
            src[...] = level
            prev, dil = dilations[g - 1], dilations[g]
            ratio, rows = dil // prev, tm // dil
            slabs = [None] * dil
            for rp in range(prev):
                for q in range(ratio):
                    slabs[rp + prev * q] = src[pl.ds(rp * (tm // prev) + q, rows, stride=ratio), :]
            level = jnp.concatenate(slabs, axis=0)
            h_ref[g, :, cs] = level.astype(BF16)


def _modulate(xf, norm_w3, mod, layer, seq, dilations):
    m, d = xf.shape
    n_copies = len(dilations)
    tm = PROJ_TM if n_copies > 1 else PROJ_TM // 2
    tiles_per_batch = seq // tm
    scratch = [pltpu.VMEM((2 * (n_copies - 1), tm, LANES), F32)] if n_copies > 1 else []
    kern = lambda *refs: _modulate_kernel(*refs, dilations=dilations)
    return pl.pallas_call(
        kern,
        grid=(m // tm,),
        in_specs=[
            pl.BlockSpec((tm, d), lambda i: (i, 0)),
            pl.BlockSpec((None, 1, d), lambda i: (layer, 0, 0)),
            pl.BlockSpec((None, None, 3, d), lambda i: (layer, i // tiles_per_batch, 0, 0)),
        ],
        out_specs=pl.BlockSpec((n_copies, tm, d), lambda i: (0, i, 0)),
        out_shape=jax.ShapeDtypeStruct((n_copies, m, d), BF16),
        scratch_shapes=scratch,
        compiler_params=_params("parallel"),
        name=f"modulate_l{layer}",
    )(xf, norm_w3, mod)


def _inproj_kernel(h_ref, w_ref, o_ref, wb_ref):
    @pl.when(pl.program_id(1) == 0)
    def _():
        wb_ref[...] = w_ref[...].astype(BF16)

    acc = jnp.dot(h_ref[...], wb_ref[...], preferred_element_type=F32)
    o_ref[...] = acc.astype(o_ref.dtype)


def _in_proj(h, w_in, layer, widx):
    n_copies, m, d = h.shape
    n = w_in.shape[-1]
    tm, tn = PROJ_TM, PROJ_TN
    tiles_per_copy = 3 * N_HEADS * HEAD_DIM // tn

    def copy_of(j):
        if n_copies == 1:
            return 0
        return jnp.where(j < tiles_per_copy * n_copies, j // tiles_per_copy, 0)

    return pl.pallas_call(
        _inproj_kernel,
        grid=(n // tn, m // tm),
        in_specs=[
            pl.BlockSpec((None, tm, d), lambda j, i: (copy_of(j), i, 0)),
            pl.BlockSpec((None, d, tn), lambda j, i: (widx, 0, j)),
        ],
        out_specs=pl.BlockSpec((tm, tn), lambda j, i: (i, j)),
        out_shape=jax.ShapeDtypeStruct((m, n), BF16),
        scratch_shapes=[pltpu.VMEM((d, tn), BF16)],
        compiler_params=_params("arbitrary", "arbitrary"),
        name=f"in_proj_l{layer}",
    )(h, w_in)


def _outproj_kernel(og_ref, w_ref, x_ref, mod_ref, *rest, final):
    if final:
        fw_ref, o_ref, wb_ref = rest
    else:
        o_ref, wb_ref = rest

    @pl.when(pl.program_id(1) == 0)
    def _():
        wb_ref[...] = w_ref[...].astype(BF16)

    y = jnp.dot(og_ref[...], wb_ref[...], preferred_element_type=F32)
    xn = x_ref[...] + mod_ref[2:3, :] * y
    if final:
        ms = jnp.mean(xn * xn, axis=-1, keepdims=True)
        xn = xn * lax.rsqrt(ms + EPS) * fw_ref[...]
    o_ref[...] = xn


def _out_proj(og, w_out, xf, mod, layer, widx, seq, final_w=None):
    m, k = og.shape
    n = w_out.shape[-1]
    final = final_w is not None
    tm, tn = (PROJ_TM // 2, n) if final else (PROJ_TM, PROJ_TN)
    tiles_per_batch = seq // tm
    w_mode = dict(pipeline_mode=pl.Buffered(1)) if final else {}
    in_specs = [
        pl.BlockSpec((tm, k), lambda j, i: (i, 0)),
        pl.BlockSpec((None, k, tn), lambda j, i: (widx, 0, j), **w_mode),
        pl.BlockSpec((tm, tn), lambda j, i: (i, j)),
        pl.BlockSpec((None, None, 3, tn), lambda j, i: (layer, i // tiles_per_batch, 0, j)),
    ]
    args = [og, w_out, xf, mod]
    if final:
        in_specs.append(pl.BlockSpec((1, n), lambda j, i: (0, 0)))
        args.append(final_w.reshape(1, n))
    kern = lambda *refs: _outproj_kernel(*refs, final=final)
    return pl.pallas_call(
        kern,
        grid=(n // tn, m // tm),
        in_specs=in_specs,
        out_specs=pl.BlockSpec((tm, tn), lambda j, i: (i, j)),
        out_shape=jax.ShapeDtypeStruct((m, n), F32),
        scratch_shapes=[pltpu.VMEM((k, tn), BF16)],
        compiler_params=_params("arbitrary", "arbitrary"),
        name=f"out_proj_l{layer}",
    )(*args)


def _nt_dot(a, b):
    return lax.dot_general(a, b, (((1,), (1,)), ((), ())), preferred_element_type=F32)


def _exp_weights(d, m):
    return jnp.exp2((d - m) * EXP2_SCALE)


def _pv_with_denominator(p, v):
    v1 = jnp.concatenate([v, jnp.ones(v.shape, v.dtype)], axis=1)
    acc = jnp.dot(p.astype(BF16), v1, preferred_element_type=F32)
    return acc[:, :HEAD_DIM], acc[:, HEAD_DIM:]


def _band_bias(slope, dilation, with_prev):
    nk = 2 * QB if with_prev else QB
    qi = lax.broadcasted_iota(jnp.int32, (QB, nk), 0)
    ki = lax.broadcasted_iota(jnp.int32, (QB, nk), 1)
    step = qi - ki + (QB if with_prev else 0)
    ok = (step >= 0) & (step <= QB)
    return jnp.where(ok, (-slope * dilation / SM_SCALE) * step.astype(F32), NEG)


def _softmax_parts(d, v):
    m = jnp.max(d, axis=-1, keepdims=True)
    acc, l = _pv_with_denominator(_exp_weights(d, m), v)
    return acc, l, jnp.broadcast_to(m, acc.shape)


def _softmax_out_lse(d, v):
    acc, l, m = _softmax_parts(d, v)
    return acc / l, m * SM_SCALE + jnp.log(l)


def _dilated_kernel(slopes_ref, q0, k0, v0, q1, k1, v1, q2, k2, v2, z_ref, o_ref,
                    out1, lse1, out2, lse2, tout2, tlse2, *, tile_rows):
    slope = slopes_ref[pl.program_id(1)]
    s_len = q0.shape[0]
    d0, d1, d2 = DILATIONS

    def rows(ref, start, size=QB):
        return ref[pl.ds(start, size), :]

    def prev_cur(ref, start_prev, start_cur):
        if start_cur == start_prev + QB:
            return rows(ref, start_prev, 2 * QB)
        return jnp.concatenate([rows(ref, start_prev), rows(ref, start_cur)], axis=0)

    per_res1 = tile_rows // d1
    per_res2 = tile_rows // d2
    n_tiles = s_len // tile_rows
    nq1 = s_len // d1 // QB

    def start1(r, n):
        l0 = n * QB
        return (l0 // per_res1) * tile_rows + r * per_res1 + l0 % per_res1

    def gather2(ref, r):
        return jnp.concatenate([rows(ref, t * tile_rows + r * per_res2, per_res2)
                                for t in range(n_tiles)], axis=0)

    bias = {(d, wp): _band_bias(slope, d, wp) for d in DILATIONS for wp in (False, True)}

    def task1(r, n):
        def kv(ref):
            return rows(ref, start1(r, 0)) if n == 0 else prev_cur(ref, start1(r, n - 1), start1(r, n))

        def finish(d):
            dst = pl.ds(n * QB * d1 + r, QB, stride=d1)
            out1[dst, :], lse1[dst, :] = _softmax_out_lse(d, kv(v1))

        return (lambda: _nt_dot(rows(q1, start1(r, n)), kv(k1)) + bias[d1, n > 0]), finish

    def task2(r):
        def finish(d):
            dst = pl.ds((r % d1) * (s_len // d1) + r // d1, QB, stride=d2 // d1)
            tout2[dst, :], tlse2[dst, :] = _softmax_out_lse(d, gather2(v2, r))

        return (lambda: _nt_dot(gather2(q2, r), gather2(k2, r)) + bias[d2, False]), finish

    def second_hop(_):
        for r in range(d1):
            src = pl.ds(r * (s_len // d1), s_len // d1)
            dst = pl.ds(r, s_len // d1, stride=d1)
            out2[dst, :] = tout2[src, :]
            lse2[dst, :] = tlse2[src, :]

    def task0(n):
        def kv(ref):
            return rows(ref, 0) if n == 0 else rows(ref, (n - 1) * QB, 2 * QB)

        def finish(d):
            a0, l0, m0 = _softmax_parts(d, kv(v0))
            blk = pl.ds(n * QB, QB)
            s0 = m0 * SM_SCALE
            s1 = lse1[blk, :]
            s2 = lse2[blk, :]
            mx = jnp.maximum(jnp.maximum(s0, s1), s2)
            w0 = jnp.exp(s0 - mx)
            w1 = jnp.exp(s1 - mx)
            w2 = jnp.exp(s2 - mx)
            num = w0 * a0 + w1 * out1[blk, :] + w2 * out2[blk, :]
            den = w0 * l0 + w1 + w2
            o_ref[blk, :] = (num / den * _silu(z_ref[blk, :].astype(F32))).astype(o_ref.dtype)

        return (lambda: _nt_dot(rows(q0, n * QB), kv(k0)) + bias[d0, n > 0]), finish

    tasks = [task1(r, n) for r in range(d1) for n in range(nq1)]
    tasks += [task2(r) for r in range(d2)] + [((lambda: None), second_hop)]
    tasks += [task0(n) for n in range(s_len // QB)]

    scores = [score() for score, _ in tasks[:SCORE_LOOKAHEAD]]
    for i, (_, finish) in enumerate(tasks):
        if i + SCORE_LOOKAHEAD < len(tasks):
            scores.append(tasks[i + SCORE_LOOKAHEAD][0]())
        finish(scores[i])


def _dilated_attention(proj, slopes, tile_rows):
    b, s, _ = proj.shape
    blk = (None, s, HEAD_DIM)

    def col(c):
        return pl.BlockSpec(blk, lambda bi, hi, sl: (bi, 0, c * N_HEADS + hi))

    grid_spec = pltpu.PrefetchScalarGridSpec(
        num_scalar_prefetch=1,
        grid=(b, N_HEADS),
        in_specs=[col(c) for c in range(3 * N_GROUPS + 1)],
        out_specs=pl.BlockSpec(blk, lambda bi, hi, sl: (bi, 0, hi)),
        scratch_shapes=[pltpu.VMEM((s, HEAD_DIM), F32) for _ in range(6)],
    )
    kern = lambda *refs: _dilated_kernel(*refs, tile_rows=tile_rows)
    return pl.pallas_call(
        kern,
        grid_spec=grid_spec,
        out_shape=jax.ShapeDtypeStruct((b, s, N_HEADS * HEAD_DIM), BF16),
        compiler_params=_params("parallel", "arbitrary"),
        name="dilated_mixer",
    )(slopes, *([proj] * (3 * N_GROUPS + 1)))


N_BIAS_PARTS = 3
GATE_ROWS = 16
MOBA_HEADS = 2


def _moba_kernel(slopes_ref, q_ref, k_ref, v_ref, z_ref, o_ref, ka_ref, va_ref):
    s_len = q_ref.shape[0]
    blk = MOBA_BLOCK
    nblk = s_len // blk
    heads = range(MOBA_HEADS)
    lanes = [slice(hh * HEAD_DIM, (hh + 1) * HEAD_DIM) for hh in heads]

    @pl.when(pl.program_id(1) == 0)
    def _():
        pos = lax.broadcasted_iota(jnp.int32, (s_len, HEAD_DIM), 0)
        lane = lax.broadcasted_iota(jnp.int32, (s_len, HEAD_DIM), 1)
        for hh in heads:
            slope = slopes_ref[pl.program_id(0) * MOBA_HEADS + hh]
            bias = pos.astype(F32) * (slope / SM_SCALE)
            aug = jnp.where(lane == pos // blk, 1.0, 0.0)
            for part in range(N_BIAS_PARTS):
                piece = bias.astype(BF16).astype(F32)
                aug = jnp.where(lane == nblk + part, piece, aug)
                bias = bias - piece
            ka_ref[hh, :, HEAD_DIM:] = aug.astype(BF16)
            va_ref[hh, :, HEAD_DIM:] = jnp.ones((s_len, HEAD_DIM), BF16)

    row = lax.broadcasted_iota(jnp.int32, (GATE_ROWS, s_len), 0)
    col = lax.broadcasted_iota(jnp.int32, (GATE_ROWS, s_len), 1)
    ind = jnp.where(col // blk == row, 1.0 / blk, 0.0).astype(BF16)
    kmean = [jnp.dot(ind, k_ref[:, lanes[hh]], preferred_element_type=F32) for hh in heads]
    gate_t = []
    for hh in heads:
        ka_ref[hh, :, :HEAD_DIM] = k_ref[:, lanes[hh]]
        va_ref[hh, :, :HEAD_DIM] = v_ref[:, lanes[hh]]
        k_hi = kmean[hh].astype(BF16)
        k_lo = (kmean[hh] - k_hi.astype(F32)).astype(BF16)
        g2 = _nt_dot(jnp.concatenate([k_hi, k_lo], axis=0), q_ref[:, lanes[hh]])
        gate_t.append((g2[:GATE_ROWS] + g2[GATE_ROWS:])[:nblk, :])

    sub = lax.broadcasted_iota(jnp.int32, (nblk, blk), 0)
    tq = lax.broadcasted_iota(jnp.int32, (blk, blk), 0)
    sk = lax.broadcasted_iota(jnp.int32, (blk, blk), 1)
    causal = jnp.where(tq >= sk, 0.0, NEG)
    aug_row = lax.broadcasted_iota(jnp.int32, (HEAD_DIM, blk), 0)
    unit_rows = jnp.where((aug_row >= nblk) & (aug_row < nblk + N_BIAS_PARTS), 1.0, 0.0)
    pad_rows = jnp.zeros((HEAD_DIM - nblk, blk), F32)

    scores = {}
    for n in range(nblk):
        rows = pl.ds(n * blk, blk)
        for hh in heads:
            g = gate_t[hh][:, n * blk:(n + 1) * blk]
            rank = jnp.zeros((nblk, blk), jnp.int32)
            for mp in range(n):
                g_mp = g[mp:mp + 1, :]
                beats = (g_mp > g) | ((g_mp == g) & (mp < sub))
                rank = rank + jnp.where(beats, 1, 0)
            allowed = ((sub < n) & (rank < MOBA_TOPK)) | (sub == n)
            mask_t = jnp.concatenate([jnp.where(allowed, 0.0, NEG), pad_rows], axis=0) + unit_rows
            qa = jnp.concatenate([q_ref[rows, lanes[hh]], mask_t.T.astype(BF16)], axis=1)
            scores[hh, n] = _nt_dot(qa, ka_ref[hh, pl.ds(0, (n + 1) * blk), :])

    for n in range(nblk):
        rows = pl.ds(n * blk, blk)
        nk = (n + 1) * blk
        for hh in heads:
            d = scores[hh, n]
            d_own = d[:, n * blk:] + causal
            m = jnp.max(d_own, axis=-1, keepdims=True)
            if n > 0:
                d_past = d[:, :n * blk]
                m = jnp.maximum(m, jnp.max(d_past, axis=-1, keepdims=True))
                p = jnp.concatenate([_exp_weights(d_past, m), _exp_weights(d_own, m)], axis=1)
            else:
                p = _exp_weights(d_own, m)
            acc = jnp.dot(p.astype(BF16), va_ref[hh, pl.ds(0, nk), :], preferred_element_type=F32)
            o = acc[:, :HEAD_DIM] / acc[:, HEAD_DIM:]
            z = z_ref[rows, lanes[hh]].astype(F32)
            o_ref[rows, lanes[hh]] = (o * _silu(z)).astype(o_ref.dtype)


def _moba_attention(proj, slopes):
    b, s, _ = proj.shape
    blk = (None, s, MOBA_HEADS * HEAD_DIM)
    n_steps = N_HEADS // MOBA_HEADS

    def col(c):
        return pl.BlockSpec(blk, lambda hi, bi, sl: (bi, 0, c * n_steps + hi))

    grid_spec = pltpu.PrefetchScalarGridSpec(
        num_scalar_prefetch=1,
        grid=(n_steps, b),
        in_specs=[col(c) for c in range(4)],
        out_specs=pl.BlockSpec(blk, lambda hi, bi, sl: (bi, 0, hi)),
        scratch_shapes=[pltpu.VMEM((MOBA_HEADS, s, 2 * HEAD_DIM), BF16),
                        pltpu.VMEM((MOBA_HEADS, s, 2 * HEAD_DIM), BF16)],
    )
    return pl.pallas_call(
        _moba_kernel,
        grid_spec=grid_spec,
        out_shape=jax.ShapeDtypeStruct((b, s, N_HEADS * HEAD_DIM), BF16),
        compiler_params=_params("arbitrary", "arbitrary"),
        name="moba_mixer",
    )(slopes, proj, proj, proj, proj)


def kernel(x, c, norm_w, mod_w, mod_b, a_w_in, a_w_out, b_w_in, b_w_out, final_norm_w):
    b, s, d = x.shape
    depth = norm_w.shape[0]
    slopes = jnp.exp2(-8.0 * jnp.arange(1, N_HEADS + 1, dtype=F32) / N_HEADS)
    mod = _modulation(c, mod_w, mod_b)
    norm_w3 = norm_w.reshape(depth, 1, d)
    xf = x.reshape(b * s, d)
    for i in range(depth):
        j = i // 2
        final_w = final_norm_w if i == depth - 1 else None
        if i % 2 == 0:
            proj = _in_proj(_modulate(xf, norm_w3, mod, i, s, DILATIONS), a_w_in, i, j)
            og = _dilated_attention(proj.reshape(b, s, -1), slopes, PROJ_TM)
            xf = _out_proj(og.reshape(b * s, -1), a_w_out, xf, mod, i, j, s, final_w)
        else:
            proj = _in_proj(_modulate(xf, norm_w3, mod, i, s, (1,)), b_w_in, i, j)
            og = _moba_attention(proj.reshape(b, s, -1), slopes)
            xf = _out_proj(og.reshape(b * s, -1), b_w_out, xf, mod, i, j, s, final_w)
    return xf.reshape(b, s, d)
```

```python
import math

import jax
import jax.numpy as jnp
from jax import lax
from jax.experimental import pallas as pl
from jax.experimental.pallas import tpu as pltpu

F32 = jnp.float32
BF16 = jnp.bfloat16

HEAD_DIM = 128
N_HEADS = 16
DSWA_PATTERNS = ((128, 1), (512, 4), (2048, 16))
DILATIONS = tuple(d for _, d in DSWA_PATTERNS)
N_GROUPS = len(DSWA_PATTERNS)
QB = 128
MOBA_BLOCK = 256
MOBA_TOPK = 3
EPS = 1e-6
NEG = -1e30
SM_SCALE = HEAD_DIM ** -0.5
EXP2_SCALE = SM_SCALE * math.log2(math.e)

LANES = 128
VMEM_LIMIT_BYTES = 56 * 1024 * 1024
MOD_PAD_ROWS = 16
PROJ_TM = 1024
PROJ_TN = 1024
SCORE_LOOKAHEAD = 4


def _params(*sem):
    return pltpu.CompilerParams(dimension_semantics=sem, vmem_limit_bytes=VMEM_LIMIT_BYTES)


def _silu(z):
    return z / (1.0 + jnp.exp(-z))


def _mod_kernel(c_ref, w_ref, b_ref, o_ref):
    cond = _silu(c_ref[...])
    acc = jnp.dot(cond.astype(BF16), w_ref[...].astype(BF16), preferred_element_type=F32)
    o_ref[...] = acc + b_ref[...]


def _modulation(c, mod_w, mod_b):
    depth, d, n = mod_w.shape
    b = c.shape[0]
    tn = n // 4
    c_pad = jnp.pad(c, ((0, MOD_PAD_ROWS - b), (0, 0)))
    out = pl.pallas_call(
        _mod_kernel,
        grid=(depth, n // tn),
        in_specs=[
            pl.BlockSpec((MOD_PAD_ROWS, d), lambda l, j: (0, 0)),
            pl.BlockSpec((None, d, tn), lambda l, j: (l, 0, j)),
            pl.BlockSpec((None, 1, tn), lambda l, j: (l, 0, j)),
        ],
        out_specs=pl.BlockSpec((None, MOD_PAD_ROWS, tn), lambda l, j: (l, 0, j)),
        out_shape=jax.ShapeDtypeStruct((depth, MOD_PAD_ROWS, n), F32),
        compiler_params=_params("parallel", "arbitrary"),
        name="adaln_modulation",
    )(c_pad, mod_w, mod_b.reshape(depth, 1, n))
    return out[:, :b].reshape(depth, b, 3, d)


def _modulate_kernel(x_ref, nw_ref, mod_ref, h_ref, *stage, dilations):
    tm, d_model = x_ref.shape
    x = x_ref[...]
    rs = lax.rsqrt(jnp.mean(x * x, axis=-1, keepdims=True) + EPS)
    for c in range(d_model // LANES):
        cs = slice(c * LANES, (c + 1) * LANES)
        y = x_ref[:, cs] * rs * nw_ref[:, cs]
        hc = y * (1.0 + mod_ref[1:2, cs]) + mod_ref[0:1, cs]
        h_ref[0, :, cs] = hc.astype(BF16)
        n_levels = len(dilations) - 1
        level = hc
        for g in range(1, n_levels + 1):
            src = stage[0].at[(g - 1) * 2 + c % 2]
            src[...] = level
            prev, dil = dilations[g - 1], dilations[g]
            ratio, rows = dil // prev, tm // dil
            slabs = [None] * dil
            for rp in range(prev):
                for q in range(ratio):
                    slabs[rp + prev * q] = src[pl.ds(rp * (tm // prev) + q, rows, stride=ratio), :]
            level = jnp.concatenate(slabs, axis=0)
            h_ref[g, :, cs] = level.astype(BF16)


def _modulate(xf, norm_w3, mod, layer, seq, dilations):
    m, d = xf.shape
    n_copies = len(dilations)
    tm = PROJ_TM if n_copies > 1 else PROJ_TM // 2
    tiles_per_batch = seq // tm
    scratch = [pltpu.VMEM((2 * (n_copies - 1), tm, LANES), F32)] if n_copies > 1 else []
    kern = lambda *refs: _modulate_kernel(*refs, dilations=dilations)
    return pl.pallas_call(
        kern,
        grid=(m // tm,),
        in_specs=[
            pl.BlockSpec((tm, d), lambda i: (i, 0)),
            pl.BlockSpec((None, 1, d), lambda i: (layer, 0, 0)),
            pl.BlockSpec((None, None, 3, d), lambda i: (layer, i // tiles_per_batch, 0, 0)),
        ],
        out_specs=pl.BlockSpec((n_copies, tm, d), lambda i: (0, i, 0)),
        out_shape=jax.ShapeDtypeStruct((n_copies, m, d), BF16),
        scratch_shapes=scratch,
        compiler_params=_params("parallel"),
        name=f"modulate_l{layer}",
    )(xf, norm_w3, mod)


def _inproj_kernel(h_ref, w_ref, o_ref, wb_ref):
    @pl.when(pl.program_id(1) == 0)
    def _():
        wb_ref[...] = w_ref[...].astype(BF16)

    acc = jnp.dot(h_ref[...], wb_ref[...], preferred_element_type=F32)
    o_ref[...] = acc.astype(o_ref.dtype)


def _in_proj(h, w_in, layer, widx):
    n_copies, m, d = h.shape
    n = w_in.shape[-1]
    tm, tn = PROJ_TM, PROJ_TN
    tiles_per_copy = 3 * N_HEADS * HEAD_DIM // tn

    def copy_of(j):
        if n_copies == 1:
            return 0
        return jnp.where(j < tiles_per_copy * n_copies, j // tiles_per_copy, 0)

    return pl.pallas_call(
        _inproj_kernel,
        grid=(n // tn, m // tm),
        in_specs=[
            pl.BlockSpec((None, tm, d), lambda j, i: (copy_of(j), i, 0)),
            pl.BlockSpec((None, d, tn), lambda j, i: (widx, 0, j)),
        ],
        out_specs=pl.BlockSpec((tm, tn), lambda j, i: (i, j)),
        out_shape=jax.ShapeDtypeStruct((m, n), BF16),
        scratch_shapes=[pltpu.VMEM((d, tn), BF16)],
        compiler_params=_params("arbitrary", "arbitrary"),
        name=f"in_proj_l{layer}",
    )(h, w_in)


def _outproj_kernel(og_ref, w_ref, x_ref, mod_ref, *rest, final):
    if final:
        fw_ref, o_ref, wb_ref = rest
    else:
        o_ref, wb_ref = rest

    @pl.when(pl.program_id(1) == 0)
    def _():
        wb_ref[...] = w_ref[...].astype(BF16)

    y = jnp.dot(og_ref[...], wb_ref[...], preferred_element_type=F32)
    xn = x_ref[...] + mod_ref[2:3, :] * y
    if final:
        ms = jnp.mean(xn * xn, axis=-1, keepdims=True)
        xn = xn * lax.rsqrt(ms + EPS) * fw_ref[...]
    o_ref[...] = xn


def _out_proj(og, w_out, xf, mod, layer, widx, seq, final_w=None):
    m, k = og.shape
    n = w_out.shape[-1]
    final = final_w is not None
    tm, tn = (PROJ_TM // 2, n) if final else (PROJ_TM, PROJ_TN)
    tiles_per_batch = seq // tm
    w_mode = dict(pipeline_mode=pl.Buffered(1)) if final else {}
    in_specs = [
        pl.BlockSpec((tm, k), lambda j, i: (i, 0)),
        pl.BlockSpec((None, k, tn), lambda j, i: (widx, 0, j), **w_mode),
        pl.BlockSpec((tm, tn), lambda j, i: (i, j)),
        pl.BlockSpec((None, None, 3, tn), lambda j, i: (layer, i // tiles_per_batch, 0, j)),
    ]
    args = [og, w_out, xf, mod]
    if final:
        in_specs.append(pl.BlockSpec((1, n), lambda j, i: (0, 0)))
        args.append(final_w.reshape(1, n))
    kern = lambda *refs: _outproj_kernel(*refs, final=final)
    return pl.pallas_call(
        kern,
        grid=(n // tn, m // tm),
        in_specs=in_specs,
        out_specs=pl.BlockSpec((tm, tn), lambda j, i: (i, j)),
        out_shape=jax.ShapeDtypeStruct((m, n), F32),
        scratch_shapes=[pltpu.VMEM((k, tn), BF16)],
        compiler_params=_params("arbitrary", "arbitrary"),
        name=f"out_proj_l{layer}",
    )(*args)


def _nt_dot(a, b):
    return lax.dot_general(a, b, (((1,), (1,)), ((), ())), preferred_element_type=F32)


def _exp_weights(d, m):
    return jnp.exp2((d - m) * EXP2_SCALE)


def _pv_with_denominator(p, v):
    v1 = jnp.concatenate([v, jnp.ones(v.shape, v.dtype)], axis=1)
    acc = jnp.dot(p.astype(BF16), v1, preferred_element_type=F32)
    return acc[:, :HEAD_DIM], acc[:, HEAD_DIM:]


def _band_bias(slope, dilation, with_prev):
    nk = 2 * QB if with_prev else QB
    qi = lax.broadcasted_iota(jnp.int32, (QB, nk), 0)
    ki = lax.broadcasted_iota(jnp.int32, (QB, nk), 1)
    step = qi - ki + (QB if with_prev else 0)
    ok = (step >= 0) & (step <= QB)
    return jnp.where(ok, (-slope * dilation / SM_SCALE) * step.astype(F32), NEG)


def _softmax_parts(d, v):
    m = jnp.max(d, axis=-1, keepdims=True)
    acc, l = _pv_with_denominator(_exp_weights(d, m), v)
    return acc, l, jnp.broadcast_to(m, acc.shape)


def _softmax_out_lse(d, v):
    acc, l, m = _softmax_parts(d, v)
    return acc / l, m * EXP2_SCALE + jnp.log2(l)


def _dilated_kernel(slopes_ref, q0, k0, v0, q1, k1, v1, q2, k2, v2, z_ref, o_ref,
                    out1, lse1, out2, lse2, tout2, tlse2, *, tile_rows):
    slope = slopes_ref[pl.program_id(1)]
    s_len = q0.shape[0]
    d0, d1, d2 = DILATIONS

    def rows(ref, start, size=QB):
        return ref[pl.ds(start, size), :]

    def prev_cur(ref, start_prev, start_cur):
        if start_cur == start_prev + QB:
            return rows(ref, start_prev, 2 * QB)
        return jnp.concatenate([rows(ref, start_prev), rows(ref, start_cur)], axis=0)

    per_res1 = tile_rows // d1
    per_res2 = tile_rows // d2
    n_tiles = s_len // tile_rows
    nq1 = s_len // d1 // QB

    def start1(r, n):
        l0 = n * QB
        return (l0 // per_res1) * tile_rows + r * per_res1 + l0 % per_res1

    def gather2(ref, r):
        return jnp.concatenate([rows(ref, t * tile_rows + r * per_res2, per_res2)
                                for t in range(n_tiles)], axis=0)

    bias = {(d, wp): _band_bias(slope, d, wp) for d in DILATIONS for wp in (False, True)}

    def task1(r, n):
        def kv(ref):
            return rows(ref, start1(r, 0)) if n == 0 else prev_cur(ref, start1(r, n - 1), start1(r, n))

        def finish(d):
            dst = pl.ds(n * QB * d1 + r, QB, stride=d1)
            out1[dst, :], lse1[dst, :] = _softmax_out_lse(d, kv(v1))

        return (lambda: _nt_dot(rows(q1, start1(r, n)), kv(k1)) + bias[d1, n > 0]), finish

    def task2(r):
        def finish(d):
            dst = pl.ds((r % d1) * (s_len // d1) + r // d1, QB, stride=d2 // d1)
            tout2[dst, :], tlse2[dst, :] = _softmax_out_lse(d, gather2(v2, r))

        return (lambda: _nt_dot(gather2(q2, r), gather2(k2, r)) + bias[d2, False]), finish

    def second_hop(_):
        for r in range(d1):
            src = pl.ds(r * (s_len // d1), s_len // d1)
            dst = pl.ds(r, s_len // d1, stride=d1)
            out2[dst, :] = tout2[src, :]
            lse2[dst, :] = tlse2[src, :]

    def task0(n):
        def kv(ref):
            return rows(ref, 0) if n == 0 else rows(ref, (n - 1) * QB, 2 * QB)

        def finish(d):
            a0, l0, m0 = _softmax_parts(d, kv(v0))
            blk = pl.ds(n * QB, QB)
            s0 = m0 * EXP2_SCALE
            s1 = lse1[blk, :]
            s2 = lse2[blk, :]
            mx = jnp.maximum(jnp.maximum(s0, s1), s2)
            w0 = jnp.exp2(s0 - mx)
            w1 = jnp.exp2(s1 - mx)
            w2 = jnp.exp2(s2 - mx)
            num = w0 * a0 + w1 * out1[blk, :] + w2 * out2[blk, :]
            den = w0 * l0 + w1 + w2
            o_ref[blk, :] = (num / den * _silu(z_ref[blk, :].astype(F32))).astype(o_ref.dtype)

        return (lambda: _nt_dot(rows(q0, n * QB), kv(k0)) + bias[d0, n > 0]), finish

    tasks = [task1(r, n) for r in range(d1) for n in range(nq1)]
    tasks += [task2(r) for r in range(d2)] + [((lambda: None), second_hop)]
    tasks += [task0(n) for n in range(s_len // QB)]

    scores = [score() for score, _ in tasks[:SCORE_LOOKAHEAD]]
    for i, (_, finish) in enumerate(tasks):
        if i + SCORE_LOOKAHEAD < len(tasks):
            scores.append(tasks[i + SCORE_LOOKAHEAD][0]())
        finish(scores[i])


def _dilated_attention(proj, slopes, tile_rows):
    b, s, _ = proj.shape
    blk = (None, s, HEAD_DIM)

    def col(c):
        return pl.BlockSpec(blk, lambda bi, hi, sl: (bi, 0, c * N_HEADS + hi))

    grid_spec = pltpu.PrefetchScalarGridSpec(
        num_scalar_prefetch=1,
        grid=(b, N_HEADS),
        in_specs=[col(c) for c in range(3 * N_GROUPS + 1)],
        out_specs=pl.BlockSpec(blk, lambda bi, hi, sl: (bi, 0, hi)),
        scratch_shapes=[pltpu.VMEM((s, HEAD_DIM), F32) for _ in range(6)],
    )
    kern = lambda *refs: _dilated_kernel(*refs, tile_rows=tile_rows)
    return pl.pallas_call(
        kern,
        grid_spec=grid_spec,
        out_shape=jax.ShapeDtypeStruct((b, s, N_HEADS * HEAD_DIM), BF16),
        compiler_params=_params("parallel", "arbitrary"),
        name="dilated_mixer",
    )(slopes, *([proj] * (3 * N_GROUPS + 1)))


N_BIAS_PARTS = 3
GATE_ROWS = 16
MOBA_HEADS = 2


def _moba_kernel(slopes_ref, q_ref, k_ref, v_ref, z_ref, o_ref, ka_ref, va_ref):
    s_len = q_ref.shape[0]
    blk = MOBA_BLOCK
    nblk = s_len // blk
    heads = range(MOBA_HEADS)
    lanes = [slice(hh * HEAD_DIM, (hh + 1) * HEAD_DIM) for hh in heads]

    @pl.when(pl.program_id(1) == 0)
    def _():
        pos = lax.broadcasted_iota(jnp.int32, (s_len, HEAD_DIM), 0)
        lane = lax.broadcasted_iota(jnp.int32, (s_len, HEAD_DIM), 1)
        for hh in heads:
            slope = slopes_ref[pl.program_id(0) * MOBA_HEADS + hh]
            bias = pos.astype(F32) * (slope / SM_SCALE)
            aug = jnp.where(lane == pos // blk, 1.0, 0.0)
            for part in range(N_BIAS_PARTS):
                piece = bias.astype(BF16).astype(F32)
                aug = jnp.where(lane == nblk + part, piece, aug)
                bias = bias - piece
            ka_ref[hh, :, HEAD_DIM:] = aug.astype(BF16)
            va_ref[hh, :, HEAD_DIM:] = jnp.ones((s_len, HEAD_DIM), BF16)

    row = lax.broadcasted_iota(jnp.int32, (GATE_ROWS, s_len), 0)
    col = lax.broadcasted_iota(jnp.int32, (GATE_ROWS, s_len), 1)
    ind = jnp.where(col // blk == row, 1.0 / blk, 0.0).astype(BF16)
    kmean = [jnp.dot(ind, k_ref[:, lanes[hh]], preferred_element_type=F32) for hh in heads]
    gate_t = []
    for hh in heads:
        ka_ref[hh, :, :HEAD_DIM] = k_ref[:, lanes[hh]]
        va_ref[hh, :, :HEAD_DIM] = v_ref[:, lanes[hh]]
        k_hi = kmean[hh].astype(BF16)
        k_lo = (kmean[hh] - k_hi.astype(F32)).astype(BF16)
        g2 = _nt_dot(jnp.concatenate([k_hi, k_lo], axis=0), q_ref[:, lanes[hh]])
        gate_t.append((g2[:GATE_ROWS] + g2[GATE_ROWS:])[:nblk, :])

    sub = lax.broadcasted_iota(jnp.int32, (nblk, blk), 0)
    tq = lax.broadcasted_iota(jnp.int32, (blk, blk), 0)
    sk = lax.broadcasted_iota(jnp.int32, (blk, blk), 1)
    causal = jnp.where(tq >= sk, 0.0, NEG)
    aug_row = lax.broadcasted_iota(jnp.int32, (HEAD_DIM, blk), 0)
    unit_rows = jnp.where((aug_row >= nblk) & (aug_row < nblk + N_BIAS_PARTS), 1.0, 0.0)
    pad_rows = jnp.zeros((HEAD_DIM - nblk, blk), F32)

    scores = {}
    for n in range(nblk):
        rows = pl.ds(n * blk, blk)
        for hh in heads:
            g = gate_t[hh][:, n * blk:(n + 1) * blk]
            rank = jnp.zeros((nblk, blk), jnp.int32)
            for mp in range(n):
                g_mp = g[mp:mp + 1, :]
                beats = (g_mp > g) | ((g_mp == g) & (mp < sub))
                rank = rank + jnp.where(beats, 1, 0)
            allowed = ((sub < n) & (rank < MOBA_TOPK)) | (sub == n)
            mask_t = jnp.concatenate([jnp.where(allowed, 0.0, NEG), pad_rows], axis=0) + unit_rows
            qa = jnp.concatenate([q_ref[rows, lanes[hh]], mask_t.T.astype(BF16)], axis=1)
            scores[hh, n] = _nt_dot(qa, ka_ref[hh, pl.ds(0, (n + 1) * blk), :])

    for n in range(nblk):
        rows = pl.ds(n * blk, blk)
        nk = (n + 1) * blk
        for hh in heads:
            d = scores[hh, n]
            d_own = d[:, n * blk:] + causal
            m = jnp.max(d_own, axis=-1, keepdims=True)
            if n > 0:
                d_past = d[:, :n * blk]
                m = jnp.maximum(m, jnp.max(d_past, axis=-1, keepdims=True))
                p = jnp.concatenate([_exp_weights(d_past, m), _exp_weights(d_own, m)], axis=1)
            else:
                p = _exp_weights(d_own, m)
            acc = jnp.dot(p.astype(BF16), va_ref[hh, pl.ds(0, nk), :], preferred_element_type=F32)
            o = acc[:, :HEAD_DIM] / acc[:, HEAD_DIM:]
            z = z_ref[rows, lanes[hh]].astype(F32)
            o_ref[rows, lanes[hh]] = (o * _silu(z)).astype(o_ref.dtype)


def _moba_attention(proj, slopes):
    b, s, _ = proj.shape
    blk = (None, s, MOBA_HEADS * HEAD_DIM)
    n_steps = N_HEADS // MOBA_HEADS

    def col(c):
        return pl.BlockSpec(blk, lambda hi, bi, sl: (bi, 0, c * n_steps + hi))

    grid_spec = pltpu.PrefetchScalarGridSpec(
        num_scalar_prefetch=1,
        grid=(n_steps, b),
        in_specs=[col(c) for c in range(4)],
        out_specs=pl.BlockSpec(blk, lambda hi, bi, sl: (bi, 0, hi)),
        scratch_shapes=[pltpu.VMEM((MOBA_HEADS, s, 2 * HEAD_DIM), BF16),
                        pltpu.VMEM((MOBA_HEADS, s, 2 * HEAD_DIM), BF16)],
    )
    return pl.pallas_call(
        _moba_kernel,
        grid_spec=grid_spec,
        out_shape=jax.ShapeDtypeStruct((b, s, N_HEADS * HEAD_DIM), BF16),
        compiler_params=_params("arbitrary", "arbitrary"),
        name="moba_mixer",
    )(slopes, proj, proj, proj, proj)


def kernel(x, c, norm_w, mod_w, mod_b, a_w_in, a_w_out, b_w_in, b_w_out, final_norm_w):
    b, s, d = x.shape
    depth = norm_w.shape[0]
    slopes = jnp.exp2(-8.0 * jnp.arange(1, N_HEADS + 1, dtype=F32) / N_HEADS)
    mod = _modulation(c, mod_w, mod_b)
    norm_w3 = norm_w.reshape(depth, 1, d)
    xf = x.reshape(b * s, d)
    for i in range(depth):
        j = i // 2
        final_w = final_norm_w if i == depth - 1 else None
        if i % 2 == 0:
            proj = _in_proj(_modulate(xf, norm_w3, mod, i, s, DILATIONS), a_w_in, i, j)
            og = _dilated_attention(proj.reshape(b, s, -1), slopes, PROJ_TM)
            xf = _out_proj(og.reshape(b * s, -1), a_w_out, xf, mod, i, j, s, final_w)
        else:
            proj = _in_proj(_modulate(xf, norm_w3, mod, i, s, (1,)), b_w_in, i, j)
            og = _moba_attention(proj.reshape(b, s, -1), slopes)
            xf = _out_proj(og.reshape(b * s, -1), b_w_out, xf, mod, i, j, s, final_w)
    return xf.reshape(b, s, d)
```

```python
import math

import jax
import jax.numpy as jnp
from jax import lax
from jax.experimental import pallas as pl
from jax.experimental.pallas import tpu as pltpu

F32 = jnp.float32
BF16 = jnp.bfloat16

HEAD_DIM = 128
N_HEADS = 16
DSWA_PATTERNS = ((128, 1), (512, 4), (2048, 16))
DILATIONS = tuple(d for _, d in DSWA_PATTERNS)
N_GROUPS = len(DSWA_PATTERNS)
QB = 128
MOBA_BLOCK = 256
MOBA_TOPK = 3
EPS = 1e-6
NEG = -1e30
SM_SCALE = HEAD_DIM ** -0.5
EXP2_SCALE = SM_SCALE * math.log2(math.e)

LANES = 128
VMEM_LIMIT_BYTES = 56 * 1024 * 1024
MOD_PAD_ROWS = 16
PROJ_TM = 1024
PROJ_TN = 1024
MOD_ROW_CHUNK = 64
SCORE_LOOKAHEAD = 4


def _params(*sem):
    return pltpu.CompilerParams(dimension_semantics=sem, vmem_limit_bytes=VMEM_LIMIT_BYTES)


def _silu(z):
    return z / (1.0 + jnp.exp(-z))


def _mod_kernel(c_ref, w_ref, b_ref, o_ref):
    cond = _silu(c_ref[...])
    acc = jnp.dot(cond.astype(BF16), w_ref[...].astype(BF16), preferred_element_type=F32)
    o_ref[...] = acc + b_ref[...]


def _modulation(c, mod_w, mod_b):
    depth, d, n = mod_w.shape
    b = c.shape[0]
    tn = n // 4
    c_pad = jnp.pad(c, ((0, MOD_PAD_ROWS - b), (0, 0)))
    out = pl.pallas_call(
        _mod_kernel,
        grid=(depth, n // tn),
        in_specs=[
            pl.BlockSpec((MOD_PAD_ROWS, d), lambda l, j: (0, 0)),
            pl.BlockSpec((None, d, tn), lambda l, j: (l, 0, j)),
            pl.BlockSpec((None, 1, tn), lambda l, j: (l, 0, j)),
        ],
        out_specs=pl.BlockSpec((None, MOD_PAD_ROWS, tn), lambda l, j: (l, 0, j)),
        out_shape=jax.ShapeDtypeStruct((depth, MOD_PAD_ROWS, n), F32),
        compiler_params=_params("parallel", "arbitrary"),
        name="adaln_modulation",
    )(c_pad, mod_w, mod_b.reshape(depth, 1, n))
    return out[:, :b].reshape(depth, b, 3, d)


def _modulate_kernel(x_ref, nw_ref, mod_ref, h_ref, *stage, dilations):
    tm, d_model = x_ref.shape
    n_levels = len(dilations) - 1
    x = x_ref[...]
    rs = lax.rsqrt(jnp.mean(x * x, axis=-1, keepdims=True) + EPS)
    for c in range(d_model // LANES):
        cs = slice(c * LANES, (c + 1) * LANES)
        stages = [stage[0].at[g * 2 + c % 2] for g in range(n_levels)]
        for r0 in range(0, tm, MOD_ROW_CHUNK):
            rc = slice(r0, r0 + MOD_ROW_CHUNK)
            y = x_ref[rc, cs] * rs[rc] * nw_ref[:, cs]
            hc = y * (1.0 + mod_ref[1:2, cs]) + mod_ref[0:1, cs]
            h_ref[0, rc, cs] = hc.astype(BF16)
            if n_levels:
                stages[0][rc, :] = hc
        for g in range(1, n_levels + 1):
            src = stages[g - 1]
            prev, dil = dilations[g - 1], dilations[g]
            ratio, rows = dil // prev, tm // dil
            for rp in range(prev):
                for q in range(ratio):
                    slab = src[pl.ds(rp * (tm // prev) + q, rows, stride=ratio), :]
                    dst = pl.ds((rp + prev * q) * rows, rows)
                    h_ref[g, dst, cs] = slab.astype(BF16)
                    if g < n_levels:
                        stages[g][dst, :] = slab


def _modulate(xf, norm_w3, mod, layer, seq, dilations):
    m, d = xf.shape
    n_copies = len(dilations)
    tm = PROJ_TM if n_copies > 1 else PROJ_TM // 2
    tiles_per_batch = seq // tm
    scratch = [pltpu.VMEM((2 * (n_copies - 1), tm, LANES), F32)] if n_copies > 1 else []
    kern = lambda *refs: _modulate_kernel(*refs, dilations=dilations)
    return pl.pallas_call(
        kern,
        grid=(m // tm,),
        in_specs=[
            pl.BlockSpec((tm, d), lambda i: (i, 0)),
            pl.BlockSpec((None, 1, d), lambda i: (layer, 0, 0)),
            pl.BlockSpec((None, None, 3, d), lambda i: (layer, i // tiles_per_batch, 0, 0)),
        ],
        out_specs=pl.BlockSpec((n_copies, tm, d), lambda i: (0, i, 0)),
        out_shape=jax.ShapeDtypeStruct((n_copies, m, d), BF16),
        scratch_shapes=scratch,
        compiler_params=_params("parallel"),
        name=f"modulate_l{layer}",
    )(xf, norm_w3, mod)


def _inproj_kernel(h_ref, w_ref, o_ref, wb_ref):
    @pl.when(pl.program_id(1) == 0)
    def _():
        wb_ref[...] = w_ref[...].astype(BF16)

    acc = jnp.dot(h_ref[...], wb_ref[...], preferred_element_type=F32)
    o_ref[...] = acc.astype(o_ref.dtype)


def _in_proj(h, w_in, layer, widx):
    n_copies, m, d = h.shape
    n = w_in.shape[-1]
    tm, tn = PROJ_TM, PROJ_TN
    tiles_per_copy = 3 * N_HEADS * HEAD_DIM // tn

    def copy_of(j):
        if n_copies == 1:
            return 0
        return jnp.where(j < tiles_per_copy * n_copies, j // tiles_per_copy, 0)

    return pl.pallas_call(
        _inproj_kernel,
        grid=(n // tn, m // tm),
        in_specs=[
            pl.BlockSpec((None, tm, d), lambda j, i: (copy_of(j), i, 0)),
            pl.BlockSpec((None, d, tn), lambda j, i: (widx, 0, j)),
        ],
        out_specs=pl.BlockSpec((tm, tn), lambda j, i: (i, j)),
        out_shape=jax.ShapeDtypeStruct((m, n), BF16),
        scratch_shapes=[pltpu.VMEM((d, tn), BF16)],
        compiler_params=_params("arbitrary", "arbitrary"),
        name=f"in_proj_l{layer}",
    )(h, w_in)


def _outproj_kernel(og_ref, w_ref, x_ref, mod_ref, *rest, final):
    if final:
        fw_ref, o_ref, wb_ref = rest
    else:
        o_ref, wb_ref = rest

    @pl.when(pl.program_id(1) == 0)
    def _():
        wb_ref[...] = w_ref[...].astype(BF16)

    y = jnp.dot(og_ref[...], wb_ref[...], preferred_element_type=F32)
    xn = x_ref[...] + mod_ref[2:3, :] * y
    if final:
        ms = jnp.mean(xn * xn, axis=-1, keepdims=True)
        xn = xn * lax.rsqrt(ms + EPS) * fw_ref[...]
    o_ref[...] = xn


def _out_proj(og, w_out, xf, mod, layer, widx, seq, final_w=None):
    m, k = og.shape
    n = w_out.shape[-1]
    final = final_w is not None
    tm, tn = (PROJ_TM // 2, n) if final else (PROJ_TM, PROJ_TN)
    tiles_per_batch = seq // tm
    w_mode = dict(pipeline_mode=pl.Buffered(1)) if final else {}
    in_specs = [
        pl.BlockSpec((tm, k), lambda j, i: (i, 0)),
        pl.BlockSpec((None, k, tn), lambda j, i: (widx, 0, j), **w_mode),
        pl.BlockSpec((tm, tn), lambda j, i: (i, j)),
        pl.BlockSpec((None, None, 3, tn), lambda j, i: (layer, i // tiles_per_batch, 0, j)),
    ]
    args = [og, w_out, xf, mod]
    if final:
        in_specs.append(pl.BlockSpec((1, n), lambda j, i: (0, 0)))
        args.append(final_w.reshape(1, n))
    kern = lambda *refs: _outproj_kernel(*refs, final=final)
    return pl.pallas_call(
        kern,
        grid=(n // tn, m // tm),
        in_specs=in_specs,
        out_specs=pl.BlockSpec((tm, tn), lambda j, i: (i, j)),
        out_shape=jax.ShapeDtypeStruct((m, n), F32),
        scratch_shapes=[pltpu.VMEM((k, tn), BF16)],
        compiler_params=_params("arbitrary", "arbitrary"),
        name=f"out_proj_l{layer}",
    )(*args)


def _nt_dot(a, b):
    return lax.dot_general(a, b, (((1,), (1,)), ((), ())), preferred_element_type=F32)


def _exp_weights(d, m):
    return jnp.exp2((d - m) * EXP2_SCALE)


def _pv_with_denominator(p, v):
    v1 = jnp.concatenate([v, jnp.ones(v.shape, v.dtype)], axis=1)
    acc = jnp.dot(p.astype(BF16), v1, preferred_element_type=F32)
    return acc[:, :HEAD_DIM], acc[:, HEAD_DIM:]


def _band_bias(slope, dilation, with_prev):
    nk = 2 * QB if with_prev else QB
    qi = lax.broadcasted_iota(jnp.int32, (QB, nk), 0)
    ki = lax.broadcasted_iota(jnp.int32, (QB, nk), 1)
    step = qi - ki + (QB if with_prev else 0)
    ok = (step >= 0) & (step <= QB)
    return jnp.where(ok, (-slope * dilation / SM_SCALE) * step.astype(F32), NEG)


def _softmax_parts(d, v):
    m = jnp.max(d, axis=-1, keepdims=True)
    acc, l = _pv_with_denominator(_exp_weights(d, m), v)
    return acc, l, jnp.broadcast_to(m, acc.shape)


def _softmax_out_lse(d, v):
    acc, l, m = _softmax_parts(d, v)
    return acc / l, m * EXP2_SCALE + jnp.log2(l)


def _dilated_kernel(slopes_ref, q0, k0, v0, q1, k1, v1, q2, k2, v2, z_ref, o_ref,
                    out1, lse1, out2, lse2, tout2, tlse2, *, tile_rows):
    slope = slopes_ref[pl.program_id(1)]
    s_len = q0.shape[0]
    d0, d1, d2 = DILATIONS

    def rows(ref, start, size=QB):
        return ref[pl.ds(start, size), :]

    def prev_cur(ref, start_prev, start_cur):
        if start_cur == start_prev + QB:
            return rows(ref, start_prev, 2 * QB)
        return jnp.concatenate([rows(ref, start_prev), rows(ref, start_cur)], axis=0)

    per_res1 = tile_rows // d1
    per_res2 = tile_rows // d2
    n_tiles = s_len // tile_rows
    nq1 = s_len // d1 // QB

    def start1(r, n):
        l0 = n * QB
        return (l0 // per_res1) * tile_rows + r * per_res1 + l0 % per_res1

    def gather2(ref, r):
        return jnp.concatenate([rows(ref, t * tile_rows + r * per_res2, per_res2)
                                for t in range(n_tiles)], axis=0)

    bias = {(d, wp): _band_bias(slope, d, wp) for d in DILATIONS for wp in (False, True)}

    def task1(r, n):
        def kv(ref):
            return rows(ref, start1(r, 0)) if n == 0 else prev_cur(ref, start1(r, n - 1), start1(r, n))

        def finish(d):
            dst = pl.ds(n * QB * d1 + r, QB, stride=d1)
            out1[dst, :], lse1[dst, :] = _softmax_out_lse(d, kv(v1))

        return (lambda: _nt_dot(rows(q1, start1(r, n)), kv(k1)) + bias[d1, n > 0]), finish

    def task2(r):
        def finish(d):
            dst = pl.ds((r % d1) * (s_len // d1) + r // d1, QB, stride=d2 // d1)
            tout2[dst, :], tlse2[dst, :] = _softmax_out_lse(d, gather2(v2, r))

        return (lambda: _nt_dot(gather2(q2, r), gather2(k2, r)) + bias[d2, False]), finish

    def second_hop(_):
        for r in range(d1):
            src = pl.ds(r * (s_len // d1), s_len // d1)
            dst = pl.ds(r, s_len // d1, stride=d1)
            out2[dst, :] = tout2[src, :]
            lse2[dst, :] = tlse2[src, :]

    def task0(n):
        def kv(ref):
            return rows(ref, 0) if n == 0 else rows(ref, (n - 1) * QB, 2 * QB)

        def finish(d):
            a0, l0, m0 = _softmax_parts(d, kv(v0))
            blk = pl.ds(n * QB, QB)
            s0 = m0 * EXP2_SCALE
            s1 = lse1[blk, :]
            s2 = lse2[blk, :]
            mx = jnp.maximum(jnp.maximum(s0, s1), s2)
            w0 = jnp.exp2(s0 - mx)
            w1 = jnp.exp2(s1 - mx)
            w2 = jnp.exp2(s2 - mx)
            num = w0 * a0 + w1 * out1[blk, :] + w2 * out2[blk, :]
            den = w0 * l0 + w1 + w2
            o_ref[blk, :] = (num / den * _silu(z_ref[blk, :].astype(F32))).astype(o_ref.dtype)

        return (lambda: _nt_dot(rows(q0, n * QB), kv(k0)) + bias[d0, n > 0]), finish

    tasks = [task1(r, n) for r in range(d1) for n in range(nq1)]
    tasks += [task2(r) for r in range(d2)] + [((lambda: None), second_hop)]
    tasks += [task0(n) for n in range(s_len // QB)]

    scores = [score() for score, _ in tasks[:SCORE_LOOKAHEAD]]
    for i, (_, finish) in enumerate(tasks):
        if i + SCORE_LOOKAHEAD < len(tasks):
            scores.append(tasks[i + SCORE_LOOKAHEAD][0]())
        finish(scores[i])


def _dilated_attention(proj, slopes, tile_rows):
    b, s, _ = proj.shape
    blk = (None, s, HEAD_DIM)

    def col(c):
        return pl.BlockSpec(blk, lambda bi, hi, sl: (bi, 0, c * N_HEADS + hi))

    grid_spec = pltpu.PrefetchScalarGridSpec(
        num_scalar_prefetch=1,
        grid=(b, N_HEADS),
        in_specs=[col(c) for c in range(3 * N_GROUPS + 1)],
        out_specs=pl.BlockSpec(blk, lambda bi, hi, sl: (bi, 0, hi)),
        scratch_shapes=[pltpu.VMEM((s, HEAD_DIM), F32) for _ in range(6)],
    )
    kern = lambda *refs: _dilated_kernel(*refs, tile_rows=tile_rows)
    return pl.pallas_call(
        kern,
        grid_spec=grid_spec,
        out_shape=jax.ShapeDtypeStruct((b, s, N_HEADS * HEAD_DIM), BF16),
        compiler_params=_params("parallel", "arbitrary"),
        name="dilated_mixer",
    )(slopes, *([proj] * (3 * N_GROUPS + 1)))


N_BIAS_PARTS = 3
GATE_ROWS = 16
MOBA_HEADS = 2


def _moba_kernel(slopes_ref, q_ref, k_ref, v_ref, z_ref, o_ref, ka_ref, va_ref):
    s_len = q_ref.shape[0]
    blk = MOBA_BLOCK
    nblk = s_len // blk
    heads = range(MOBA_HEADS)
    lanes = [slice(hh * HEAD_DIM, (hh + 1) * HEAD_DIM) for hh in heads]

    @pl.when(pl.program_id(1) == 0)
    def _():
        pos = lax.broadcasted_iota(jnp.int32, (s_len, HEAD_DIM), 0)
        lane = lax.broadcasted_iota(jnp.int32, (s_len, HEAD_DIM), 1)
        for hh in heads:
            slope = slopes_ref[pl.program_id(0) * MOBA_HEADS + hh]
            bias = pos.astype(F32) * (slope / SM_SCALE)
            aug = jnp.where(lane == pos // blk, 1.0, 0.0)
            for part in range(N_BIAS_PARTS):
                piece = bias.astype(BF16).astype(F32)
                aug = jnp.where(lane == nblk + part, piece, aug)
                bias = bias - piece
            ka_ref[hh, :, HEAD_DIM:] = aug.astype(BF16)
            va_ref[hh, :, HEAD_DIM:] = jnp.ones((s_len, HEAD_DIM), BF16)

    row = lax.broadcasted_iota(jnp.int32, (GATE_ROWS, s_len), 0)
    col = lax.broadcasted_iota(jnp.int32, (GATE_ROWS, s_len), 1)
    ind = jnp.where(col // blk == row, 1.0 / blk, 0.0).astype(BF16)
    kmean = [jnp.dot(ind, k_ref[:, lanes[hh]], preferred_element_type=F32) for hh in heads]
    gate_t = []
    for hh in heads:
        ka_ref[hh, :, :HEAD_DIM] = k_ref[:, lanes[hh]]
        va_ref[hh, :, :HEAD_DIM] = v_ref[:, lanes[hh]]
        k_hi = kmean[hh].astype(BF16)
        k_lo = (kmean[hh] - k_hi.astype(F32)).astype(BF16)
        g2 = _nt_dot(jnp.concatenate([k_hi, k_lo], axis=0), q_ref[:, lanes[hh]])
        gate_t.append((g2[:GATE_ROWS] + g2[GATE_ROWS:])[:nblk, :])

    sub = lax.broadcasted_iota(jnp.int32, (nblk, blk), 0)
    tq = lax.broadcasted_iota(jnp.int32, (blk, blk), 0)
    sk = lax.broadcasted_iota(jnp.int32, (blk, blk), 1)
    causal = jnp.where(tq >= sk, 0.0, NEG)
    aug_row = lax.broadcasted_iota(jnp.int32, (HEAD_DIM, blk), 0)
    unit_rows = jnp.where((aug_row >= nblk) & (aug_row < nblk + N_BIAS_PARTS), 1.0, 0.0)
    pad_rows = jnp.zeros((HEAD_DIM - nblk, blk), F32)

    scores = {}
    for n in range(nblk):
        rows = pl.ds(n * blk, blk)
        for hh in heads:
            g = gate_t[hh][:, n * blk:(n + 1) * blk]
            rank = jnp.zeros((nblk, blk), jnp.int32)
            for mp in range(n):
                g_mp = g[mp:mp + 1, :]
                beats = (g_mp > g) | ((g_mp == g) & (mp < sub))
                rank = rank + jnp.where(beats, 1, 0)
            allowed = ((sub < n) & (rank < MOBA_TOPK)) | (sub == n)
            mask_t = jnp.concatenate([jnp.where(allowed, 0.0, NEG), pad_rows], axis=0) + unit_rows
            qa = jnp.concatenate([q_ref[rows, lanes[hh]], mask_t.T.astype(BF16)], axis=1)
            scores[hh, n] = _nt_dot(qa, ka_ref[hh, pl.ds(0, (n + 1) * blk), :])

    for n in range(nblk):
        rows = pl.ds(n * blk, blk)
        nk = (n + 1) * blk
        for hh in heads:
            d = scores[hh, n]
            d_own = d[:, n * blk:] + causal
            m = jnp.max(d_own, axis=-1, keepdims=True)
            if n > 0:
                d_past = d[:, :n * blk]
                m = jnp.maximum(m, jnp.max(d_past, axis=-1, keepdims=True))
                p = jnp.concatenate([_exp_weights(d_past, m), _exp_weights(d_own, m)], axis=1)
            else:
                p = _exp_weights(d_own, m)
            acc = jnp.dot(p.astype(BF16), va_ref[hh, pl.ds(0, nk), :], preferred_element_type=F32)
            o = acc[:, :HEAD_DIM] / acc[:, HEAD_DIM:]
            z = z_ref[rows, lanes[hh]].astype(F32)
            o_ref[rows, lanes[hh]] = (o * _silu(z)).astype(o_ref.dtype)


def _moba_attention(proj, slopes):
    b, s, _ = proj.shape
    blk = (None, s, MOBA_HEADS * HEAD_DIM)
    n_steps = N_HEADS // MOBA_HEADS

    def col(c):
        return pl.BlockSpec(blk, lambda hi, bi, sl: (bi, 0, c * n_steps + hi))

    grid_spec = pltpu.PrefetchScalarGridSpec(
        num_scalar_prefetch=1,
        grid=(n_steps, b),
        in_specs=[col(c) for c in range(4)],
        out_specs=pl.BlockSpec(blk, lambda hi, bi, sl: (bi, 0, hi)),
        scratch_shapes=[pltpu.VMEM((MOBA_HEADS, s, 2 * HEAD_DIM), BF16),
                        pltpu.VMEM((MOBA_HEADS, s, 2 * HEAD_DIM), BF16)],
    )
    return pl.pallas_call(
        _moba_kernel,
        grid_spec=grid_spec,
        out_shape=jax.ShapeDtypeStruct((b, s, N_HEADS * HEAD_DIM), BF16),
        compiler_params=_params("arbitrary", "arbitrary"),
        name="moba_mixer",
    )(slopes, proj, proj, proj, proj)


def kernel(x, c, norm_w, mod_w, mod_b, a_w_in, a_w_out, b_w_in, b_w_out, final_norm_w):
    b, s, d = x.shape
    depth = norm_w.shape[0]
    slopes = jnp.exp2(-8.0 * jnp.arange(1, N_HEADS + 1, dtype=F32) / N_HEADS)
    mod = _modulation(c, mod_w, mod_b)
    norm_w3 = norm_w.reshape(depth, 1, d)
    xf = x.reshape(b * s, d)
    for i in range(depth):
        j = i // 2
        final_w = final_norm_w if i == depth - 1 else None
        if i % 2 == 0:
            proj = _in_proj(_modulate(xf, norm_w3, mod, i, s, DILATIONS), a_w_in, i, j)
            og = _dilated_attention(proj.reshape(b, s, -1), slopes, PROJ_TM)
            xf = _out_proj(og.reshape(b * s, -1), a_w_out, xf, mod, i, j, s, final_w)
        else:
            proj = _in_proj(_modulate(xf, norm_w3, mod, i, s, (1,)), b_w_in, i, j)
            og = _moba_attention(proj.reshape(b, s, -1), slopes)
            xf = _out_proj(og.reshape(b * s, -1), b_w_out, xf, mod, i, j, s, final_w)
    return xf.reshape(b, s, d)
```

```python
import math

import jax
import jax.numpy as jnp
from jax import lax
from jax.experimental import pallas as pl
from jax.experimental.pallas import tpu as pltpu

F32 = jnp.float32
BF16 = jnp.bfloat16

HEAD_DIM = 128
N_HEADS = 16
DSWA_PATTERNS = ((128, 1), (512, 4), (2048, 16))
DILATIONS = tuple(d for _, d in DSWA_PATTERNS)
N_GROUPS = len(DSWA_PATTERNS)
QB = 128
MOBA_BLOCK = 256
MOBA_TOPK = 3
EPS = 1e-6
NEG = -1e30
SM_SCALE = HEAD_DIM ** -0.5
EXP2_SCALE = SM_SCALE * math.log2(math.e)

LANES = 128
VMEM_LIMIT_BYTES = 56 * 1024 * 1024
MOD_PAD_ROWS = 16
PROJ_TM = 1024
PROJ_TN = 1024
MOD_ROW_CHUNK = 64
SCORE_LOOKAHEAD = 4


def _params(*sem):
    return pltpu.CompilerParams(dimension_semantics=sem, vmem_limit_bytes=VMEM_LIMIT_BYTES)


def _silu(z):
    return z / (1.0 + jnp.exp(-z))


def _mod_kernel(c_ref, w_ref, b_ref, o_ref):
    cond = _silu(c_ref[...])
    acc = jnp.dot(cond.astype(BF16), w_ref[...].astype(BF16), preferred_element_type=F32)
    o_ref[...] = acc + b_ref[...]


def _modulation(c, mod_w, mod_b):
    depth, d, n = mod_w.shape
    b = c.shape[0]
    tn = n // 4
    c_pad = jnp.pad(c, ((0, MOD_PAD_ROWS - b), (0, 0)))
    out = pl.pallas_call(
        _mod_kernel,
        grid=(depth, n // tn),
        in_specs=[
            pl.BlockSpec((MOD_PAD_ROWS, d), lambda l, j: (0, 0)),
            pl.BlockSpec((None, d, tn), lambda l, j: (l, 0, j)),
            pl.BlockSpec((None, 1, tn), lambda l, j: (l, 0, j)),
        ],
        out_specs=pl.BlockSpec((None, MOD_PAD_ROWS, tn), lambda l, j: (l, 0, j)),
        out_shape=jax.ShapeDtypeStruct((depth, MOD_PAD_ROWS, n), F32),
        compiler_params=_params("parallel", "arbitrary"),
        name="adaln_modulation",
    )(c_pad, mod_w, mod_b.reshape(depth, 1, n))
    return out[:, :b].reshape(depth, b, 3, d)


def _modulate_kernel(x_ref, nw_ref, mod_ref, h_ref, *stage, dilations):
    tm, d_model = x_ref.shape
    n_levels = len(dilations) - 1
    x = x_ref[...]
    rs = lax.rsqrt(jnp.mean(x * x, axis=-1, keepdims=True) + EPS)
    for c in range(d_model // LANES):
        cs = slice(c * LANES, (c + 1) * LANES)
        stages = [stage[0].at[g * 2 + c % 2] for g in range(n_levels)]
        for r0 in range(0, tm, MOD_ROW_CHUNK):
            rc = slice(r0, r0 + MOD_ROW_CHUNK)
            y = x_ref[rc, cs] * rs[rc] * nw_ref[:, cs]
            hc = y * (1.0 + mod_ref[1:2, cs]) + mod_ref[0:1, cs]
            h_ref[0, rc, cs] = hc.astype(BF16)
            if n_levels:
                stages[0][rc, :] = hc
        for g in range(1, n_levels + 1):
            src = stages[g - 1]
            prev, dil = dilations[g - 1], dilations[g]
            ratio, rows = dil // prev, tm // dil
            for rp in range(prev):
                for q in range(ratio):
                    slab = src[pl.ds(rp * (tm // prev) + q, rows, stride=ratio), :]
                    dst = pl.ds((rp + prev * q) * rows, rows)
                    h_ref[g, dst, cs] = slab.astype(BF16)
                    if g < n_levels:
                        stages[g][dst, :] = slab


def _modulate(xf, norm_w3, mod, layer, seq, dilations):
    m, d = xf.shape
    n_copies = len(dilations)
    tm = PROJ_TM if n_copies > 1 else PROJ_TM // 2
    tiles_per_batch = seq // tm
    scratch = [pltpu.VMEM((2 * (n_copies - 1), tm, LANES), F32)] if n_copies > 1 else []
    kern = lambda *refs: _modulate_kernel(*refs, dilations=dilations)
    return pl.pallas_call(
        kern,
        grid=(m // tm,),
        in_specs=[
            pl.BlockSpec((tm, d), lambda i: (i, 0)),
            pl.BlockSpec((None, 1, d), lambda i: (layer, 0, 0)),
            pl.BlockSpec((None, None, 3, d), lambda i: (layer, i // tiles_per_batch, 0, 0)),
        ],
        out_specs=pl.BlockSpec((n_copies, tm, d), lambda i: (0, i, 0)),
        out_shape=jax.ShapeDtypeStruct((n_copies, m, d), BF16),
        scratch_shapes=scratch,
        compiler_params=_params("parallel"),
        name=f"modulate_l{layer}",
    )(xf, norm_w3, mod)


def _inproj_kernel(h_ref, w_ref, o_ref, wb_ref):
    @pl.when(pl.program_id(1) == 0)
    def _():
        wb_ref[...] = w_ref[...].astype(BF16)

    acc = jnp.dot(h_ref[...], wb_ref[...], preferred_element_type=F32)
    o_ref[...] = acc.astype(o_ref.dtype)


def _in_proj(h, w_in, layer, widx):
    n_copies, m, d = h.shape
    n = w_in.shape[-1]
    tm, tn = PROJ_TM, PROJ_TN
    tiles_per_copy = 3 * N_HEADS * HEAD_DIM // tn

    def copy_of(j):
        if n_copies == 1:
            return 0
        return jnp.where(j < tiles_per_copy * n_copies, j // tiles_per_copy, 0)

    return pl.pallas_call(
        _inproj_kernel,
        grid=(n // tn, m // tm),
        in_specs=[
            pl.BlockSpec((None, tm, d), lambda j, i: (copy_of(j), i, 0)),
            pl.BlockSpec((None, d, tn), lambda j, i: (widx, 0, j)),
        ],
        out_specs=pl.BlockSpec((tm, tn), lambda j, i: (i, j)),
        out_shape=jax.ShapeDtypeStruct((m, n), BF16),
        scratch_shapes=[pltpu.VMEM((d, tn), BF16)],
        compiler_params=_params("arbitrary", "arbitrary"),
        name=f"in_proj_l{layer}",
    )(h, w_in)


def _outproj_kernel(og_ref, w_ref, x_ref, mod_ref, *rest, final):
    if final:
        fw_ref, o_ref, wb_ref = rest
    else:
        o_ref, wb_ref = rest

    @pl.when(pl.program_id(1) == 0)
    def _():
        wb_ref[...] = w_ref[...].astype(BF16)

    y = jnp.dot(og_ref[...], wb_ref[...], preferred_element_type=F32)
    xn = x_ref[...] + mod_ref[2:3, :] * y
    if final:
        ms = jnp.mean(xn * xn, axis=-1, keepdims=True)
        xn = xn * lax.rsqrt(ms + EPS) * fw_ref[...]
    o_ref[...] = xn


def _out_proj(og, w_out, xf, mod, layer, widx, seq, final_w=None):
    m, k = og.shape
    n = w_out.shape[-1]
    final = final_w is not None
    tm, tn = (PROJ_TM // 2, n) if final else (PROJ_TM, PROJ_TN)
    tiles_per_batch = seq // tm
    w_mode = dict(pipeline_mode=pl.Buffered(1)) if final else {}
    in_specs = [
        pl.BlockSpec((tm, k), lambda j, i: (i, 0)),
        pl.BlockSpec((None, k, tn), lambda j, i: (widx, 0, j), **w_mode),
        pl.BlockSpec((tm, tn), lambda j, i: (i, j)),
        pl.BlockSpec((None, None, 3, tn), lambda j, i: (layer, i // tiles_per_batch, 0, j)),
    ]
    args = [og, w_out, xf, mod]
    if final:
        in_specs.append(pl.BlockSpec((1, n), lambda j, i: (0, 0)))
        args.append(final_w.reshape(1, n))
    kern = lambda *refs: _outproj_kernel(*refs, final=final)
    return pl.pallas_call(
        kern,
        grid=(n // tn, m // tm),
        in_specs=in_specs,
        out_specs=pl.BlockSpec((tm, tn), lambda j, i: (i, j)),
        out_shape=jax.ShapeDtypeStruct((m, n), F32),
        scratch_shapes=[pltpu.VMEM((k, tn), BF16)],
        compiler_params=_params("arbitrary", "arbitrary"),
        name=f"out_proj_l{layer}",
    )(*args)


def _nt_dot(a, b):
    return lax.dot_general(a, b, (((1,), (1,)), ((), ())), preferred_element_type=F32)


def _exp_weights(d, m):
    return jnp.exp2((d - m) * EXP2_SCALE)


def _pv_with_denominator(p, v):
    v1 = jnp.concatenate([v, jnp.ones(v.shape, v.dtype)], axis=1)
    acc = jnp.dot(p.astype(BF16), v1, preferred_element_type=F32)
    return acc[:, :HEAD_DIM], acc[:, HEAD_DIM:]


def _band_bias(slope, dilation, with_prev):
    nk = 2 * QB if with_prev else QB
    qi = lax.broadcasted_iota(jnp.int32, (QB, nk), 0)
    ki = lax.broadcasted_iota(jnp.int32, (QB, nk), 1)
    step = qi - ki + (QB if with_prev else 0)
    ok = (step >= 0) & (step <= QB)
    return jnp.where(ok, (-slope * dilation / SM_SCALE) * step.astype(F32), NEG)


def _softmax_parts(d, v):
    m = jnp.max(d, axis=-1, keepdims=True)
    acc, l = _pv_with_denominator(_exp_weights(d, m), v)
    return acc, l, jnp.broadcast_to(m, acc.shape)


def _softmax_out_lse(d, v):
    acc, l, m = _softmax_parts(d, v)
    return acc / l, m * EXP2_SCALE + jnp.log2(l)


def _dilated_kernel(slopes_ref, q0, k0, v0, q1, k1, v1, q2, k2, v2, z_ref, o_ref,
                    out1, lse1, out2, lse2, tout2, tlse2, *, tile_rows):
    slope = slopes_ref[pl.program_id(1)]
    s_len = q0.shape[0]
    d0, d1, d2 = DILATIONS

    def rows(ref, start, size=QB):
        return ref[pl.ds(start, size), :]

    def prev_cur(ref, start_prev, start_cur):
        if start_cur == start_prev + QB:
            return rows(ref, start_prev, 2 * QB)
        return jnp.concatenate([rows(ref, start_prev), rows(ref, start_cur)], axis=0)

    per_res1 = tile_rows // d1
    per_res2 = tile_rows // d2
    n_tiles = s_len // tile_rows
    nq1 = s_len // d1 // QB

    def start1(r, n):
        l0 = n * QB
        return (l0 // per_res1) * tile_rows + r * per_res1 + l0 % per_res1

    def gather2(ref, r):
        return jnp.concatenate([rows(ref, t * tile_rows + r * per_res2, per_res2)
                                for t in range(n_tiles)], axis=0)

    bias = {(d, wp): _band_bias(slope, d, wp) for d in DILATIONS for wp in (False, True)}

    def task1(r, n):
        def kv(ref):
            return rows(ref, start1(r, 0)) if n == 0 else prev_cur(ref, start1(r, n - 1), start1(r, n))

        def finish(d):
            dst = pl.ds(n * QB * d1 + r, QB, stride=d1)
            out1[dst, :], lse1[dst, :] = _softmax_out_lse(d, kv(v1))

        return (lambda: _nt_dot(rows(q1, start1(r, n)), kv(k1)) + bias[d1, n > 0]), finish

    def task2(r):
        def finish(d):
            dst = pl.ds((r % d1) * (s_len // d1) + r // d1, QB, stride=d2 // d1)
            tout2[dst, :], tlse2[dst, :] = _softmax_out_lse(d, gather2(v2, r))

        return (lambda: _nt_dot(gather2(q2, r), gather2(k2, r)) + bias[d2, False]), finish

    def second_hop(_):
        for r in range(d1):
            src = pl.ds(r * (s_len // d1), s_len // d1)
            dst = pl.ds(r, s_len // d1, stride=d1)
            out2[dst, :] = tout2[src, :]
            lse2[dst, :] = tlse2[src, :]

    def task0(n):
        def kv(ref):
            return rows(ref, 0) if n == 0 else rows(ref, (n - 1) * QB, 2 * QB)

        def finish(d):
            a0, l0, m0 = _softmax_parts(d, kv(v0))
            blk = pl.ds(n * QB, QB)
            s0 = m0 * EXP2_SCALE
            s1 = lse1[blk, :]
            s2 = lse2[blk, :]
            mx = jnp.maximum(jnp.maximum(s0, s1), s2)
            w0 = jnp.exp2(s0 - mx)
            w1 = jnp.exp2(s1 - mx)
            w2 = jnp.exp2(s2 - mx)
            num = w0 * a0 + w1 * out1[blk, :] + w2 * out2[blk, :]
            den = w0 * l0 + w1 + w2
            o_ref[blk, :] = (num / den * _silu(z_ref[blk, :].astype(F32))).astype(o_ref.dtype)

        return (lambda: _nt_dot(rows(q0, n * QB), kv(k0)) + bias[d0, n > 0]), finish

    tasks = [task1(r, n) for r in range(d1) for n in range(nq1)]
    tasks += [task2(r) for r in range(d2)] + [((lambda: None), second_hop)]
    tasks += [task0(n) for n in range(s_len // QB)]

    scores = [score() for score, _ in tasks[:SCORE_LOOKAHEAD]]
    for i, (_, finish) in enumerate(tasks):
        if i + SCORE_LOOKAHEAD < len(tasks):
            scores.append(tasks[i + SCORE_LOOKAHEAD][0]())
        finish(scores[i])


def _dilated_attention(proj, slopes, tile_rows):
    b, s, _ = proj.shape
    blk = (None, s, HEAD_DIM)

    def col(c):
        return pl.BlockSpec(blk, lambda bi, hi, sl: (bi, 0, c * N_HEADS + hi))

    grid_spec = pltpu.PrefetchScalarGridSpec(
        num_scalar_prefetch=1,
        grid=(b, N_HEADS),
        in_specs=[col(c) for c in range(3 * N_GROUPS + 1)],
        out_specs=pl.BlockSpec(blk, lambda bi, hi, sl: (bi, 0, hi)),
        scratch_shapes=[pltpu.VMEM((s, HEAD_DIM), F32) for _ in range(6)],
    )
    kern = lambda *refs: _dilated_kernel(*refs, tile_rows=tile_rows)
    return pl.pallas_call(
        kern,
        grid_spec=grid_spec,
        out_shape=jax.ShapeDtypeStruct((b, s, N_HEADS * HEAD_DIM), BF16),
        compiler_params=_params("parallel", "arbitrary"),
        name="dilated_mixer",
    )(slopes, *([proj] * (3 * N_GROUPS + 1)))


N_BIAS_PARTS = 3
GATE_ROWS = 16
MOBA_HEADS = 2
MOBA_ROW_CHUNK = 64


def _moba_kernel(slopes_ref, q_ref, k_ref, v_ref, z_ref, o_ref, ka_ref, va_ref):
    s_len = q_ref.shape[0]
    blk = MOBA_BLOCK
    nblk = s_len // blk
    heads = range(MOBA_HEADS)
    lanes = [slice(hh * HEAD_DIM, (hh + 1) * HEAD_DIM) for hh in heads]

    @pl.when(pl.program_id(1) == 0)
    def _():
        pos = lax.broadcasted_iota(jnp.int32, (s_len, HEAD_DIM), 0)
        lane = lax.broadcasted_iota(jnp.int32, (s_len, HEAD_DIM), 1)
        for hh in heads:
            slope = slopes_ref[pl.program_id(0) * MOBA_HEADS + hh]
            bias = pos.astype(F32) * (slope / SM_SCALE)
            aug = jnp.where(lane == pos // blk, 1.0, 0.0)
            for part in range(N_BIAS_PARTS):
                piece = bias.astype(BF16).astype(F32)
                aug = jnp.where(lane == nblk + part, piece, aug)
                bias = bias - piece
            ka_ref[hh, :, HEAD_DIM:] = aug.astype(BF16)
            va_ref[hh, :, HEAD_DIM:] = jnp.ones((s_len, HEAD_DIM), BF16)

    row = lax.broadcasted_iota(jnp.int32, (GATE_ROWS, s_len), 0)
    col = lax.broadcasted_iota(jnp.int32, (GATE_ROWS, s_len), 1)
    ind = jnp.where(col // blk == row, 1.0 / blk, 0.0).astype(BF16)
    kmean = [jnp.dot(ind, k_ref[:, lanes[hh]], preferred_element_type=F32) for hh in heads]
    gate_t = []
    for hh in heads:
        ka_ref[hh, :, :HEAD_DIM] = k_ref[:, lanes[hh]]
        va_ref[hh, :, :HEAD_DIM] = v_ref[:, lanes[hh]]
        k_hi = kmean[hh].astype(BF16)
        k_lo = (kmean[hh] - k_hi.astype(F32)).astype(BF16)
        g2 = _nt_dot(jnp.concatenate([k_hi, k_lo], axis=0), q_ref[:, lanes[hh]])
        gate_t.append((g2[:GATE_ROWS] + g2[GATE_ROWS:])[:nblk, :])

    sub = lax.broadcasted_iota(jnp.int32, (nblk, blk), 0)
    tq = lax.broadcasted_iota(jnp.int32, (blk, blk), 0)
    sk = lax.broadcasted_iota(jnp.int32, (blk, blk), 1)
    causal = jnp.where(tq >= sk, 0.0, NEG)
    aug_row = lax.broadcasted_iota(jnp.int32, (HEAD_DIM, blk), 0)
    unit_rows = jnp.where((aug_row >= nblk) & (aug_row < nblk + N_BIAS_PARTS), 1.0, 0.0)
    pad_rows = jnp.zeros((HEAD_DIM - nblk, blk), F32)

    scores = {}
    for n in range(nblk):
        rows = pl.ds(n * blk, blk)
        for hh in heads:
            g = gate_t[hh][:, n * blk:(n + 1) * blk]
            rank = jnp.zeros((nblk, blk), jnp.int32)
            for mp in range(n):
                g_mp = g[mp:mp + 1, :]
                beats = (g_mp > g) | ((g_mp == g) & (mp < sub))
                rank = rank + jnp.where(beats, 1, 0)
            allowed = ((sub < n) & (rank < MOBA_TOPK)) | (sub == n)
            mask_t = jnp.concatenate([jnp.where(allowed, 0.0, NEG), pad_rows], axis=0) + unit_rows
            qa = jnp.concatenate([q_ref[rows, lanes[hh]], mask_t.T.astype(BF16)], axis=1)
            scores[hh, n] = _nt_dot(qa, ka_ref[hh, pl.ds(0, (n + 1) * blk), :])

    for n in range(nblk):
        rows = pl.ds(n * blk, blk)
        nk = (n + 1) * blk
        for hh in heads:
            d = scores[hh, n]
            parts = []
            for r0 in range(0, blk, MOBA_ROW_CHUNK):
                rc = slice(r0, r0 + MOBA_ROW_CHUNK)
                d_own = d[rc, n * blk:] + causal[rc]
                m = jnp.max(d_own, axis=-1, keepdims=True)
                if n > 0:
                    d_past = d[rc, :n * blk]
                    m = jnp.maximum(m, jnp.max(d_past, axis=-1, keepdims=True))
                    pc = jnp.concatenate([_exp_weights(d_past, m), _exp_weights(d_own, m)], axis=1)
                else:
                    pc = _exp_weights(d_own, m)
                parts.append(pc.astype(BF16))
            p = jnp.concatenate(parts, axis=0)
            acc = jnp.dot(p, va_ref[hh, pl.ds(0, nk), :], preferred_element_type=F32)
            o = acc[:, :HEAD_DIM] / acc[:, HEAD_DIM:]
            z = z_ref[rows, lanes[hh]].astype(F32)
            o_ref[rows, lanes[hh]] = (o * _silu(z)).astype(o_ref.dtype)


def _moba_attention(proj, slopes):
    b, s, _ = proj.shape
    blk = (None, s, MOBA_HEADS * HEAD_DIM)
    n_steps = N_HEADS // MOBA_HEADS

    def col(c):
        return pl.BlockSpec(blk, lambda hi, bi, sl: (bi, 0, c * n_steps + hi))

    grid_spec = pltpu.PrefetchScalarGridSpec(
        num_scalar_prefetch=1,
        grid=(n_steps, b),
        in_specs=[col(c) for c in range(4)],
        out_specs=pl.BlockSpec(blk, lambda hi, bi, sl: (bi, 0, hi)),
        scratch_shapes=[pltpu.VMEM((MOBA_HEADS, s, 2 * HEAD_DIM), BF16),
                        pltpu.VMEM((MOBA_HEADS, s, 2 * HEAD_DIM), BF16)],
    )
    return pl.pallas_call(
        _moba_kernel,
        grid_spec=grid_spec,
        out_shape=jax.ShapeDtypeStruct((b, s, N_HEADS * HEAD_DIM), BF16),
        compiler_params=_params("arbitrary", "arbitrary"),
        name="moba_mixer",
    )(slopes, proj, proj, proj, proj)


def kernel(x, c, norm_w, mod_w, mod_b, a_w_in, a_w_out, b_w_in, b_w_out, final_norm_w):
    b, s, d = x.shape
    depth = norm_w.shape[0]
    slopes = jnp.exp2(-8.0 * jnp.arange(1, N_HEADS + 1, dtype=F32) / N_HEADS)
    mod = _modulation(c, mod_w, mod_b)
    norm_w3 = norm_w.reshape(depth, 1, d)
    xf = x.reshape(b * s, d)
    for i in range(depth):
        j = i // 2
        final_w = final_norm_w if i == depth - 1 else None
        if i % 2 == 0:
            proj = _in_proj(_modulate(xf, norm_w3, mod, i, s, DILATIONS), a_w_in, i, j)
            og = _dilated_attention(proj.reshape(b, s, -1), slopes, PROJ_TM)
            xf = _out_proj(og.reshape(b * s, -1), a_w_out, xf, mod, i, j, s, final_w)
        else:
            proj = _in_proj(_modulate(xf, norm_w3, mod, i, s, (1,)), b_w_in, i, j)
            og = _moba_attention(proj.reshape(b, s, -1), slopes)
            xf = _out_proj(og.reshape(b * s, -1), b_w_out, xf, mod, i, j, s, final_w)
    return xf.reshape(b, s, d)
```

```python
import math

import jax
import jax.numpy as jnp
from jax import lax
from jax.experimental import pallas as pl
from jax.experimental.pallas import tpu as pltpu

F32 = jnp.float32
BF16 = jnp.bfloat16

HEAD_DIM = 128
N_HEADS = 16
DSWA_PATTERNS = ((128, 1), (512, 4), (2048, 16))
DILATIONS = tuple(d for _, d in DSWA_PATTERNS)
N_GROUPS = len(DSWA_PATTERNS)
QB = 128
MOBA_BLOCK = 256
MOBA_TOPK = 3
EPS = 1e-6
NEG = -1e30
SM_SCALE = HEAD_DIM ** -0.5
EXP2_SCALE = SM_SCALE * math.log2(math.e)

LANES = 128
VMEM_LIMIT_BYTES = 56 * 1024 * 1024
MOD_PAD_ROWS = 16
PROJ_TM = 1024
PROJ_TN = 1024
MOD_TM = 512
MOD_ROW_CHUNK = 64
SCORE_LOOKAHEAD = 4


def _params(*sem):
    return pltpu.CompilerParams(dimension_semantics=sem, vmem_limit_bytes=VMEM_LIMIT_BYTES)


def _silu(z):
    return z / (1.0 + jnp.exp(-z))


def _mod_kernel(c_ref, w_ref, b_ref, o_ref):
    cond = _silu(c_ref[...])
    acc = jnp.dot(cond.astype(BF16), w_ref[...].astype(BF16), preferred_element_type=F32)
    o_ref[...] = acc + b_ref[...]


def _modulation(c, mod_w, mod_b):
    depth, d, n = mod_w.shape
    b = c.shape[0]
    tn = n // 4
    c_pad = jnp.pad(c, ((0, MOD_PAD_ROWS - b), (0, 0)))
    out = pl.pallas_call(
        _mod_kernel,
        grid=(depth, n // tn),
        in_specs=[
            pl.BlockSpec((MOD_PAD_ROWS, d), lambda l, j: (0, 0)),
            pl.BlockSpec((None, d, tn), lambda l, j: (l, 0, j)),
            pl.BlockSpec((None, 1, tn), lambda l, j: (l, 0, j)),
        ],
        out_specs=pl.BlockSpec((None, MOD_PAD_ROWS, tn), lambda l, j: (l, 0, j)),
        out_shape=jax.ShapeDtypeStruct((depth, MOD_PAD_ROWS, n), F32),
        compiler_params=_params("parallel", "arbitrary"),
        name="adaln_modulation",
    )(c_pad, mod_w, mod_b.reshape(depth, 1, n))
    return out[:, :b].reshape(depth, b, 3, d)


def _modulate_kernel(x_ref, nw_ref, mod_ref, h_ref, *stage, dilations):
    tm, d_model = x_ref.shape
    n_levels = len(dilations) - 1
    x = x_ref[...]
    rs = lax.rsqrt(jnp.mean(x * x, axis=-1, keepdims=True) + EPS)
    for c in range(d_model // LANES):
        cs = slice(c * LANES, (c + 1) * LANES)
        stages = [stage[0].at[g * 2 + c % 2] for g in range(n_levels)]
        for r0 in range(0, tm, MOD_ROW_CHUNK):
            rc = slice(r0, r0 + MOD_ROW_CHUNK)
            y = x_ref[rc, cs] * rs[rc] * nw_ref[:, cs]
            hc = y * (1.0 + mod_ref[1:2, cs]) + mod_ref[0:1, cs]
            h_ref[0, rc, cs] = hc.astype(BF16)
            if n_levels:
                stages[0][rc, :] = hc
        for g in range(1, n_levels + 1):
            src = stages[g - 1]
            prev, dil = dilations[g - 1], dilations[g]
            ratio, rows = dil // prev, tm // dil
            for rp in range(prev):
                for q in range(ratio):
                    slab = src[pl.ds(rp * (tm // prev) + q, rows, stride=ratio), :]
                    dst = pl.ds((rp + prev * q) * rows, rows)
                    h_ref[g, dst, cs] = slab.astype(BF16)
                    if g < n_levels:
                        stages[g][dst, :] = slab


def _modulate(xf, norm_w3, mod, layer, seq, dilations):
    m, d = xf.shape
    n_copies = len(dilations)
    tm = MOD_TM
    tiles_per_batch = seq // tm
    scratch = [pltpu.VMEM((2 * (n_copies - 1), tm, LANES), F32)] if n_copies > 1 else []
    kern = lambda *refs: _modulate_kernel(*refs, dilations=dilations)
    return pl.pallas_call(
        kern,
        grid=(m // tm,),
        in_specs=[
            pl.BlockSpec((tm, d), lambda i: (i, 0)),
            pl.BlockSpec((None, 1, d), lambda i: (layer, 0, 0)),
            pl.BlockSpec((None, None, 3, d), lambda i: (layer, i // tiles_per_batch, 0, 0)),
        ],
        out_specs=pl.BlockSpec((n_copies, tm, d), lambda i: (0, i, 0)),
        out_shape=jax.ShapeDtypeStruct((n_copies, m, d), BF16),
        scratch_shapes=scratch,
        compiler_params=_params("parallel"),
        name=f"modulate_l{layer}",
    )(xf, norm_w3, mod)


def _inproj_kernel(h_ref, w_ref, o_ref, wb_ref):
    @pl.when(pl.program_id(1) == 0)
    def _():
        wb_ref[...] = w_ref[...].astype(BF16)

    acc = jnp.dot(h_ref[...], wb_ref[...], preferred_element_type=F32)
    o_ref[...] = acc.astype(o_ref.dtype)


def _in_proj(h, w_in, layer, widx):
    n_copies, m, d = h.shape
    n = w_in.shape[-1]
    tm, tn = PROJ_TM, PROJ_TN
    tiles_per_copy = 3 * N_HEADS * HEAD_DIM // tn

    def copy_of(j):
        if n_copies == 1:
            return 0
        return jnp.where(j < tiles_per_copy * n_copies, j // tiles_per_copy, 0)

    return pl.pallas_call(
        _inproj_kernel,
        grid=(n // tn, m // tm),
        in_specs=[
            pl.BlockSpec((None, tm, d), lambda j, i: (copy_of(j), i, 0)),
            pl.BlockSpec((None, d, tn), lambda j, i: (widx, 0, j)),
        ],
        out_specs=pl.BlockSpec((tm, tn), lambda j, i: (i, j)),
        out_shape=jax.ShapeDtypeStruct((m, n), BF16),
        scratch_shapes=[pltpu.VMEM((d, tn), BF16)],
        compiler_params=_params("arbitrary", "arbitrary"),
        name=f"in_proj_l{layer}",
    )(h, w_in)


def _outproj_kernel(og_ref, w_ref, x_ref, mod_ref, *rest, final):
    if final:
        fw_ref, o_ref, wb_ref = rest
    else:
        o_ref, wb_ref = rest

    @pl.when(pl.program_id(1) == 0)
    def _():
        wb_ref[...] = w_ref[...].astype(BF16)

    y = jnp.dot(og_ref[...], wb_ref[...], preferred_element_type=F32)
    xn = x_ref[...] + mod_ref[2:3, :] * y
    if final:
        ms = jnp.mean(xn * xn, axis=-1, keepdims=True)
        xn = xn * lax.rsqrt(ms + EPS) * fw_ref[...]
    o_ref[...] = xn


def _out_proj(og, w_out, xf, mod, layer, widx, seq, final_w=None):
    m, k = og.shape
    n = w_out.shape[-1]
    final = final_w is not None
    tm, tn = (PROJ_TM // 2, n) if final else (PROJ_TM, PROJ_TN)
    tiles_per_batch = seq // tm
    w_mode = dict(pipeline_mode=pl.Buffered(1)) if final else {}
    in_specs = [
        pl.BlockSpec((tm, k), lambda j, i: (i, 0)),
        pl.BlockSpec((None, k, tn), lambda j, i: (widx, 0, j), **w_mode),
        pl.BlockSpec((tm, tn), lambda j, i: (i, j)),
        pl.BlockSpec((None, None, 3, tn), lambda j, i: (layer, i // tiles_per_batch, 0, j)),
    ]
    args = [og, w_out, xf, mod]
    if final:
        in_specs.append(pl.BlockSpec((1, n), lambda j, i: (0, 0)))
        args.append(final_w.reshape(1, n))
    kern = lambda *refs: _outproj_kernel(*refs, final=final)
    return pl.pallas_call(
        kern,
        grid=(n // tn, m // tm),
        in_specs=in_specs,
        out_specs=pl.BlockSpec((tm, tn), lambda j, i: (i, j)),
        out_shape=jax.ShapeDtypeStruct((m, n), F32),
        scratch_shapes=[pltpu.VMEM((k, tn), BF16)],
        compiler_params=_params("arbitrary", "arbitrary"),
        name=f"out_proj_l{layer}",
    )(*args)


def _nt_dot(a, b):
    return lax.dot_general(a, b, (((1,), (1,)), ((), ())), preferred_element_type=F32)


def _exp_weights(d, m):
    return jnp.exp2((d - m) * EXP2_SCALE)


def _pv_with_denominator(p, v):
    v1 = jnp.concatenate([v, jnp.ones(v.shape, v.dtype)], axis=1)
    acc = jnp.dot(p.astype(BF16), v1, preferred_element_type=F32)
    return acc[:, :HEAD_DIM], acc[:, HEAD_DIM:]


def _band_bias(slope, dilation, with_prev):
    nk = 2 * QB if with_prev else QB
    qi = lax.broadcasted_iota(jnp.int32, (QB, nk), 0)
    ki = lax.broadcasted_iota(jnp.int32, (QB, nk), 1)
    step = qi - ki + (QB if with_prev else 0)
    ok = (step >= 0) & (step <= QB)
    return jnp.where(ok, (-slope * dilation / SM_SCALE) * step.astype(F32), NEG)


def _softmax_parts(d, v):
    m = jnp.max(d, axis=-1, keepdims=True)
    acc, l = _pv_with_denominator(_exp_weights(d, m), v)
    return acc, l, jnp.broadcast_to(m, acc.shape)


def _softmax_out_lse(d, v):
    acc, l, m = _softmax_parts(d, v)
    return acc / l, m * EXP2_SCALE + jnp.log2(l)


def _dilated_kernel(slopes_ref, q0, k0, v0, q1, k1, v1, q2, k2, v2, z_ref, o_ref,
                    out1, lse1, out2, lse2, tout2, tlse2, *, tile_rows):
    slope = slopes_ref[pl.program_id(1)]
    s_len = q0.shape[0]
    d0, d1, d2 = DILATIONS

    def rows(ref, start, size=QB):
        return ref[pl.ds(start, size), :]

    def prev_cur(ref, start_prev, start_cur):
        if start_cur == start_prev + QB:
            return rows(ref, start_prev, 2 * QB)
        return jnp.concatenate([rows(ref, start_prev), rows(ref, start_cur)], axis=0)

    per_res1 = tile_rows // d1
    per_res2 = tile_rows // d2
    n_tiles = s_len // tile_rows
    nq1 = s_len // d1 // QB

    def start1(r, n):
        l0 = n * QB
        return (l0 // per_res1) * tile_rows + r * per_res1 + l0 % per_res1

    def gather2(ref, r):
        return jnp.concatenate([rows(ref, t * tile_rows + r * per_res2, per_res2)
                                for t in range(n_tiles)], axis=0)

    bias = {(d, wp): _band_bias(slope, d, wp) for d in DILATIONS for wp in (False, True)}

    def task1(r, n):
        def kv(ref):
            return rows(ref, start1(r, 0)) if n == 0 else prev_cur(ref, start1(r, n - 1), start1(r, n))

        def finish(d):
            dst = pl.ds(n * QB * d1 + r, QB, stride=d1)
            out1[dst, :], lse1[dst, :] = _softmax_out_lse(d, kv(v1))

        return (lambda: _nt_dot(rows(q1, start1(r, n)), kv(k1)) + bias[d1, n > 0]), finish

    def task2(r):
        def finish(d):
            dst = pl.ds((r % d1) * (s_len // d1) + r // d1, QB, stride=d2 // d1)
            tout2[dst, :], tlse2[dst, :] = _softmax_out_lse(d, gather2(v2, r))

        return (lambda: _nt_dot(gather2(q2, r), gather2(k2, r)) + bias[d2, False]), finish

    def second_hop(_):
        for r in range(d1):
            src = pl.ds(r * (s_len // d1), s_len // d1)
            dst = pl.ds(r, s_len // d1, stride=d1)
            out2[dst, :] = tout2[src, :]
            lse2[dst, :] = tlse2[src, :]

    def task0(n):
        def kv(ref):
            return rows(ref, 0) if n == 0 else rows(ref, (n - 1) * QB, 2 * QB)

        def finish(d):
            a0, l0, m0 = _softmax_parts(d, kv(v0))
            blk = pl.ds(n * QB, QB)
            s0 = m0 * EXP2_SCALE
            s1 = lse1[blk, :]
            s2 = lse2[blk, :]
            mx = jnp.maximum(jnp.maximum(s0, s1), s2)
            w0 = jnp.exp2(s0 - mx)
            w1 = jnp.exp2(s1 - mx)
            w2 = jnp.exp2(s2 - mx)
            num = w0 * a0 + w1 * out1[blk, :] + w2 * out2[blk, :]
            den = w0 * l0 + w1 + w2
            o_ref[blk, :] = (num / den * _silu(z_ref[blk, :].astype(F32))).astype(o_ref.dtype)

        return (lambda: _nt_dot(rows(q0, n * QB), kv(k0)) + bias[d0, n > 0]), finish

    tasks = [task1(r, n) for r in range(d1) for n in range(nq1)]
    tasks += [task2(r) for r in range(d2)] + [((lambda: None), second_hop)]
    tasks += [task0(n) for n in range(s_len // QB)]

    scores = [score() for score, _ in tasks[:SCORE_LOOKAHEAD]]
    for i, (_, finish) in enumerate(tasks):
        if i + SCORE_LOOKAHEAD < len(tasks):
            scores.append(tasks[i + SCORE_LOOKAHEAD][0]())
        finish(scores[i])


def _dilated_attention(proj, slopes, tile_rows):
    b, s, _ = proj.shape
    blk = (None, s, HEAD_DIM)

    def col(c):
        return pl.BlockSpec(blk, lambda bi, hi, sl: (bi, 0, c * N_HEADS + hi))

    grid_spec = pltpu.PrefetchScalarGridSpec(
        num_scalar_prefetch=1,
        grid=(b, N_HEADS),
        in_specs=[col(c) for c in range(3 * N_GROUPS + 1)],
        out_specs=pl.BlockSpec(blk, lambda bi, hi, sl: (bi, 0, hi)),
        scratch_shapes=[pltpu.VMEM((s, HEAD_DIM), F32) for _ in range(6)],
    )
    kern = lambda *refs: _dilated_kernel(*refs, tile_rows=tile_rows)
    return pl.pallas_call(
        kern,
        grid_spec=grid_spec,
        out_shape=jax.ShapeDtypeStruct((b, s, N_HEADS * HEAD_DIM), BF16),
        compiler_params=_params("parallel", "arbitrary"),
        name="dilated_mixer",
    )(slopes, *([proj] * (3 * N_GROUPS + 1)))


N_BIAS_PARTS = 3
GATE_ROWS = 16
MOBA_HEADS = 2
MOBA_ROW_CHUNK = 64


def _moba_kernel(slopes_ref, q_ref, k_ref, v_ref, z_ref, o_ref, ka_ref, va_ref):
    s_len = q_ref.shape[0]
    blk = MOBA_BLOCK
    nblk = s_len // blk
    heads = range(MOBA_HEADS)
    lanes = [slice(hh * HEAD_DIM, (hh + 1) * HEAD_DIM) for hh in heads]

    @pl.when(pl.program_id(1) == 0)
    def _():
        pos = lax.broadcasted_iota(jnp.int32, (s_len, HEAD_DIM), 0)
        lane = lax.broadcasted_iota(jnp.int32, (s_len, HEAD_DIM), 1)
        for hh in heads:
            slope = slopes_ref[pl.program_id(0) * MOBA_HEADS + hh]
            bias = pos.astype(F32) * (slope / SM_SCALE)
            aug = jnp.where(lane == pos // blk, 1.0, 0.0)
            for part in range(N_BIAS_PARTS):
                piece = bias.astype(BF16).astype(F32)
                aug = jnp.where(lane == nblk + part, piece, aug)
                bias = bias - piece
            ka_ref[hh, :, HEAD_DIM:] = aug.astype(BF16)
            va_ref[hh, :, HEAD_DIM:] = jnp.ones((s_len, HEAD_DIM), BF16)

    row = lax.broadcasted_iota(jnp.int32, (GATE_ROWS, s_len), 0)
    col = lax.broadcasted_iota(jnp.int32, (GATE_ROWS, s_len), 1)
    ind = jnp.where(col // blk == row, 1.0 / blk, 0.0).astype(BF16)
    kmean = [jnp.dot(ind, k_ref[:, lanes[hh]], preferred_element_type=F32) for hh in heads]
    gate_t = []
    for hh in heads:
        ka_ref[hh, :, :HEAD_DIM] = k_ref[:, lanes[hh]]
        va_ref[hh, :, :HEAD_DIM] = v_ref[:, lanes[hh]]
        k_hi = kmean[hh].astype(BF16)
        k_lo = (kmean[hh] - k_hi.astype(F32)).astype(BF16)
        g2 = _nt_dot(jnp.concatenate([k_hi, k_lo], axis=0), q_ref[:, lanes[hh]])
        gate_t.append((g2[:GATE_ROWS] + g2[GATE_ROWS:])[:nblk, :])

    sub = lax.broadcasted_iota(jnp.int32, (nblk, blk), 0)
    tq = lax.broadcasted_iota(jnp.int32, (blk, blk), 0)
    sk = lax.broadcasted_iota(jnp.int32, (blk, blk), 1)
    causal = jnp.where(tq >= sk, 0.0, NEG)
    aug_row = lax.broadcasted_iota(jnp.int32, (HEAD_DIM, blk), 0)
    unit_rows = jnp.where((aug_row >= nblk) & (aug_row < nblk + N_BIAS_PARTS), 1.0, 0.0)
    pad_rows = jnp.zeros((HEAD_DIM - nblk, blk), F32)

    scores = {}
    for n in range(nblk):
        rows = pl.ds(n * blk, blk)
        for hh in heads:
            g = gate_t[hh][:, n * blk:(n + 1) * blk]
            rank = jnp.zeros((nblk, blk), jnp.int32)
            for mp in range(n):
                g_mp = g[mp:mp + 1, :]
                beats = (g_mp > g) | ((g_mp == g) & (mp < sub))
                rank = rank + jnp.where(beats, 1, 0)
            allowed = ((sub < n) & (rank < MOBA_TOPK)) | (sub == n)
            mask_t = jnp.concatenate([jnp.where(allowed, 0.0, NEG), pad_rows], axis=0) + unit_rows
            qa = jnp.concatenate([q_ref[rows, lanes[hh]], mask_t.T.astype(BF16)], axis=1)
            scores[hh, n] = _nt_dot(qa, ka_ref[hh, pl.ds(0, (n + 1) * blk), :])

    for n in range(nblk):
        rows = pl.ds(n * blk, blk)
        nk = (n + 1) * blk
        for hh in heads:
            d = scores[hh, n]
            parts = []
            for r0 in range(0, blk, MOBA_ROW_CHUNK):
                rc = slice(r0, r0 + MOBA_ROW_CHUNK)
                d_own = d[rc, n * blk:] + causal[rc]
                m = jnp.max(d_own, axis=-1, keepdims=True)
                if n > 0:
                    d_past = d[rc, :n * blk]
                    m = jnp.maximum(m, jnp.max(d_past, axis=-1, keepdims=True))
                    pc = jnp.concatenate([_exp_weights(d_past, m), _exp_weights(d_own, m)], axis=1)
                else:
                    pc = _exp_weights(d_own, m)
                parts.append(pc.astype(BF16))
            p = jnp.concatenate(parts, axis=0)
            acc = jnp.dot(p, va_ref[hh, pl.ds(0, nk), :], preferred_element_type=F32)
            o = acc[:, :HEAD_DIM] / acc[:, HEAD_DIM:]
            z = z_ref[rows, lanes[hh]].astype(F32)
            o_ref[rows, lanes[hh]] = (o * _silu(z)).astype(o_ref.dtype)


def _moba_attention(proj, slopes):
    b, s, _ = proj.shape
    blk = (None, s, MOBA_HEADS * HEAD_DIM)
    n_steps = N_HEADS // MOBA_HEADS

    def col(c):
        return pl.BlockSpec(blk, lambda hi, bi, sl: (bi, 0, c * n_steps + hi))

    grid_spec = pltpu.PrefetchScalarGridSpec(
        num_scalar_prefetch=1,
        grid=(n_steps, b),
        in_specs=[col(c) for c in range(4)],
        out_specs=pl.BlockSpec(blk, lambda hi, bi, sl: (bi, 0, hi)),
        scratch_shapes=[pltpu.VMEM((MOBA_HEADS, s, 2 * HEAD_DIM), BF16),
                        pltpu.VMEM((MOBA_HEADS, s, 2 * HEAD_DIM), BF16)],
    )
    return pl.pallas_call(
        _moba_kernel,
        grid_spec=grid_spec,
        out_shape=jax.ShapeDtypeStruct((b, s, N_HEADS * HEAD_DIM), BF16),
        compiler_params=_params("arbitrary", "arbitrary"),
        name="moba_mixer",
    )(slopes, proj, proj, proj, proj)


def kernel(x, c, norm_w, mod_w, mod_b, a_w_in, a_w_out, b_w_in, b_w_out, final_norm_w):
    b, s, d = x.shape
    depth = norm_w.shape[0]
    slopes = jnp.exp2(-8.0 * jnp.arange(1, N_HEADS + 1, dtype=F32) / N_HEADS)
    mod = _modulation(c, mod_w, mod_b)
    norm_w3 = norm_w.reshape(depth, 1, d)
    xf = x.reshape(b * s, d)
    for i in range(depth):
        j = i // 2
        final_w = final_norm_w if i == depth - 1 else None
        if i % 2 == 0:
            proj = _in_proj(_modulate(xf, norm_w3, mod, i, s, DILATIONS), a_w_in, i, j)
            og = _dilated_attention(proj.reshape(b, s, -1), slopes, MOD_TM)
            xf = _out_proj(og.reshape(b * s, -1), a_w_out, xf, mod, i, j, s, final_w)
        else:
            proj = _in_proj(_modulate(xf, norm_w3, mod, i, s, (1,)), b_w_in, i, j)
            og = _moba_attention(proj.reshape(b, s, -1), slopes)
            xf = _out_proj(og.reshape(b * s, -1), b_w_out, xf, mod, i, j, s, final_w)
    return xf.reshape(b, s, d)
```

```python
import functools
import math

import jax
import jax.numpy as jnp
from jax import lax
from jax.experimental import pallas as pl
from jax.experimental.pallas import tpu as pltpu

F32 = jnp.float32
BF16 = jnp.bfloat16

HEAD_DIM = 128
N_HEADS = 16
DSWA_PATTERNS = ((128, 1), (512, 4), (2048, 16))
DILATIONS = tuple(d for _, d in DSWA_PATTERNS)
N_GROUPS = len(DSWA_PATTERNS)
QB = 128
MOBA_BLOCK = 256
MOBA_TOPK = 3
EPS = 1e-6
NEG = -1e30
SM_SCALE = HEAD_DIM ** -0.5
EXP2_SCALE = SM_SCALE * math.log2(math.e)

LANES = 128
VMEM_LIMIT_BYTES = 56 * 1024 * 1024
MOD_PAD_ROWS = 16
PROJ_TM = 1024
PROJ_TN = 1024
MOD_TM = 512
MOD_ROW_CHUNK = 64
SCORE_LOOKAHEAD = 4


def _params(*sem):
    return pltpu.CompilerParams(dimension_semantics=sem, vmem_limit_bytes=VMEM_LIMIT_BYTES)


def _silu(z):
    return z / (1.0 + jnp.exp(-z))


def _mod_kernel(c_ref, w_ref, b_ref, o_ref):
    cond = _silu(c_ref[...])
    acc = jnp.dot(cond.astype(BF16), w_ref[...].astype(BF16), preferred_element_type=F32)
    o_ref[...] = acc + b_ref[...]


def _modulation(c, mod_w, mod_b):
    depth, d, n = mod_w.shape
    b = c.shape[0]
    tn = n // 4
    c_pad = jnp.pad(c, ((0, MOD_PAD_ROWS - b), (0, 0)))
    out = pl.pallas_call(
        _mod_kernel,
        grid=(depth, n // tn),
        in_specs=[
            pl.BlockSpec((MOD_PAD_ROWS, d), lambda l, j: (0, 0)),
            pl.BlockSpec((None, d, tn), lambda l, j: (l, 0, j)),
            pl.BlockSpec((None, 1, tn), lambda l, j: (l, 0, j)),
        ],
        out_specs=pl.BlockSpec((None, MOD_PAD_ROWS, tn), lambda l, j: (l, 0, j)),
        out_shape=jax.ShapeDtypeStruct((depth, MOD_PAD_ROWS, n), F32),
        compiler_params=_params("parallel", "arbitrary"),
        name="adaln_modulation",
    )(c_pad, mod_w, mod_b.reshape(depth, 1, n))
    return out[:, :b].reshape(depth, b, 3, d)


def _modulate_kernel(x_ref, nw_ref, mod_ref, h_ref, *stage, dilations):
    tm, d_model = x_ref.shape
    n_levels = len(dilations) - 1
    x = x_ref[...]
    rs = lax.rsqrt(jnp.mean(x * x, axis=-1, keepdims=True) + EPS)
    for c in range(d_model // LANES):
        cs = slice(c * LANES, (c + 1) * LANES)
        stages = [stage[0].at[g * 2 + c % 2] for g in range(n_levels)]
        for r0 in range(0, tm, MOD_ROW_CHUNK):
            rc = slice(r0, r0 + MOD_ROW_CHUNK)
            y = x_ref[rc, cs] * rs[rc] * nw_ref[:, cs]
            hc = y * (1.0 + mod_ref[1:2, cs]) + mod_ref[0:1, cs]
            h_ref[0, rc, cs] = hc.astype(BF16)
            if n_levels:
                stages[0][rc, :] = hc
        for g in range(1, n_levels + 1):
            src = stages[g - 1]
            prev, dil = dilations[g - 1], dilations[g]
            ratio, rows = dil // prev, tm // dil
            for rp in range(prev):
                for q in range(ratio):
                    slab = src[pl.ds(rp * (tm // prev) + q, rows, stride=ratio), :]
                    dst = pl.ds((rp + prev * q) * rows, rows)
                    h_ref[g, dst, cs] = slab.astype(BF16)
                    if g < n_levels:
                        stages[g][dst, :] = slab


def _modulate(xf, norm_w3, mod, layer, seq, dilations):
    m, d = xf.shape
    n_copies = len(dilations)
    tm = MOD_TM
    tiles_per_batch = seq // tm
    scratch = [pltpu.VMEM((2 * (n_copies - 1), tm, LANES), F32)] if n_copies > 1 else []
    kern = functools.partial(_modulate_kernel, dilations=dilations)
    return pl.pallas_call(
        kern,
        grid=(m // tm,),
        in_specs=[
            pl.BlockSpec((tm, d), lambda i: (i, 0)),
            pl.BlockSpec((None, 1, d), lambda i: (layer, 0, 0)),
            pl.BlockSpec((None, None, 3, d), lambda i: (layer, i // tiles_per_batch, 0, 0)),
        ],
        out_specs=pl.BlockSpec((n_copies, tm, d), lambda i: (0, i, 0)),
        out_shape=jax.ShapeDtypeStruct((n_copies, m, d), BF16),
        scratch_shapes=scratch,
        compiler_params=_params("parallel"),
        name=f"modulate_l{layer}",
    )(xf, norm_w3, mod)


def _inproj_kernel(h_ref, w_ref, o_ref, wb_ref):
    @pl.when(pl.program_id(1) == 0)
    def _():
        wb_ref[...] = w_ref[...].astype(BF16)

    acc = jnp.dot(h_ref[...], wb_ref[...], preferred_element_type=F32)
    o_ref[...] = acc.astype(o_ref.dtype)


def _in_proj(h, w_in, layer, widx):
    n_copies, m, d = h.shape
    n = w_in.shape[-1]
    tm, tn = PROJ_TM, PROJ_TN
    tiles_per_copy = 3 * N_HEADS * HEAD_DIM // tn

    def copy_of(j):
        if n_copies == 1:
            return 0
        return jnp.where(j < tiles_per_copy * n_copies, j // tiles_per_copy, 0)

    return pl.pallas_call(
        _inproj_kernel,
        grid=(n // tn, m // tm),
        in_specs=[
            pl.BlockSpec((None, tm, d), lambda j, i: (copy_of(j), i, 0)),
            pl.BlockSpec((None, d, tn), lambda j, i: (widx, 0, j)),
        ],
        out_specs=pl.BlockSpec((tm, tn), lambda j, i: (i, j)),
        out_shape=jax.ShapeDtypeStruct((m, n), BF16),
        scratch_shapes=[pltpu.VMEM((d, tn), BF16)],
        compiler_params=_params("arbitrary", "arbitrary"),
        name=f"in_proj_l{layer}",
    )(h, w_in)


def _outproj_kernel(og_ref, w_ref, x_ref, mod_ref, *rest, final):
    if final:
        fw_ref, o_ref, wb_ref = rest
    else:
        o_ref, wb_ref = rest

    @pl.when(pl.program_id(1) == 0)
    def _():
        wb_ref[...] = w_ref[...].astype(BF16)

    y = jnp.dot(og_ref[...], wb_ref[...], preferred_element_type=F32)
    xn = x_ref[...] + mod_ref[2:3, :] * y
    if final:
        ms = jnp.mean(xn * xn, axis=-1, keepdims=True)
        xn = xn * lax.rsqrt(ms + EPS) * fw_ref[...]
    o_ref[...] = xn


def _out_proj(og, w_out, xf, mod, layer, widx, seq, final_w=None):
    m, k = og.shape
    n = w_out.shape[-1]
    final = final_w is not None
    tm, tn = (PROJ_TM // 2, n) if final else (PROJ_TM, PROJ_TN)
    tiles_per_batch = seq // tm
    w_mode = dict(pipeline_mode=pl.Buffered(1)) if final else {}
    in_specs = [
        pl.BlockSpec((tm, k), lambda j, i: (i, 0)),
        pl.BlockSpec((None, k, tn), lambda j, i: (widx, 0, j), **w_mode),
        pl.BlockSpec((tm, tn), lambda j, i: (i, j)),
        pl.BlockSpec((None, None, 3, tn), lambda j, i: (layer, i // tiles_per_batch, 0, j)),
    ]
    args = [og, w_out, xf, mod]
    if final:
        in_specs.append(pl.BlockSpec((1, n), lambda j, i: (0, 0)))
        args.append(final_w.reshape(1, n))
    kern = functools.partial(_outproj_kernel, final=final)
    return pl.pallas_call(
        kern,
        grid=(n // tn, m // tm),
        in_specs=in_specs,
        out_specs=pl.BlockSpec((tm, tn), lambda j, i: (i, j)),
        out_shape=jax.ShapeDtypeStruct((m, n), F32),
        scratch_shapes=[pltpu.VMEM((k, tn), BF16)],
        compiler_params=_params("arbitrary", "arbitrary"),
        name=f"out_proj_l{layer}",
    )(*args)


def _nt_dot(a, b):
    return lax.dot_general(a, b, (((1,), (1,)), ((), ())), preferred_element_type=F32)


def _exp_weights(d, m):
    return jnp.exp2((d - m) * EXP2_SCALE)


def _pv_with_denominator(p, v):
    v1 = jnp.concatenate([v, jnp.ones(v.shape, v.dtype)], axis=1)
    acc = jnp.dot(p.astype(BF16), v1, preferred_element_type=F32)
    return acc[:, :HEAD_DIM], acc[:, HEAD_DIM:]


def _band_bias(slope, dilation, with_prev):
    nk = 2 * QB if with_prev else QB
    qi = lax.broadcasted_iota(jnp.int32, (QB, nk), 0)
    ki = lax.broadcasted_iota(jnp.int32, (QB, nk), 1)
    step = qi - ki + (QB if with_prev else 0)
    ok = (step >= 0) & (step <= QB)
    return jnp.where(ok, (-slope * dilation / SM_SCALE) * step.astype(F32), NEG)


def _softmax_parts(d, v):
    m = jnp.max(d, axis=-1, keepdims=True)
    acc, l = _pv_with_denominator(_exp_weights(d, m), v)
    return acc, l, jnp.broadcast_to(m, acc.shape)


def _softmax_out_lse(d, v):
    acc, l, m = _softmax_parts(d, v)
    return acc / l, m * EXP2_SCALE + jnp.log2(l)


def _dilated_kernel(slopes_ref, q0, k0, v0, q1, k1, v1, q2, k2, v2, z_ref, o_ref,
                    out1, lse1, out2, lse2, tout2, tlse2, *, tile_rows):
    slope = slopes_ref[pl.program_id(1)]
    s_len = q0.shape[0]
    d0, d1, d2 = DILATIONS

    def rows(ref, start, size=QB):
        return ref[pl.ds(start, size), :]

    def prev_cur(ref, start_prev, start_cur):
        if start_cur == start_prev + QB:
            return rows(ref, start_prev, 2 * QB)
        return jnp.concatenate([rows(ref, start_prev), rows(ref, start_cur)], axis=0)

    per_res1 = tile_rows // d1
    per_res2 = tile_rows // d2
    n_tiles = s_len // tile_rows
    nq1 = s_len // d1 // QB

    def start1(r, n):
        l0 = n * QB
        return (l0 // per_res1) * tile_rows + r * per_res1 + l0 % per_res1

    def gather2(ref, r):
        return jnp.concatenate([rows(ref, t * tile_rows + r * per_res2, per_res2)
                                for t in range(n_tiles)], axis=0)

    bias = {(d, wp): _band_bias(slope, d, wp) for d in DILATIONS for wp in (False, True)}

    def task1(r, n):
        def kv(ref):
            return rows(ref, start1(r, 0)) if n == 0 else prev_cur(ref, start1(r, n - 1), start1(r, n))

        def finish(d):
            dst = pl.ds(n * QB * d1 + r, QB, stride=d1)
            out1[dst, :], lse1[dst, :] = _softmax_out_lse(d, kv(v1))

        return (lambda: _nt_dot(rows(q1, start1(r, n)), kv(k1)) + bias[d1, n > 0]), finish

    def task2(r):
        def finish(d):
            dst = pl.ds((r % d1) * (s_len // d1) + r // d1, QB, stride=d2 // d1)
            tout2[dst, :], tlse2[dst, :] = _softmax_out_lse(d, gather2(v2, r))

        return (lambda: _nt_dot(gather2(q2, r), gather2(k2, r)) + bias[d2, False]), finish

    def second_hop(_):
        for r in range(d1):
            src = pl.ds(r * (s_len // d1), s_len // d1)
            dst = pl.ds(r, s_len // d1, stride=d1)
            out2[dst, :] = tout2[src, :]
            lse2[dst, :] = tlse2[src, :]

    def task0(n):
        def kv(ref):
            return rows(ref, 0) if n == 0 else rows(ref, (n - 1) * QB, 2 * QB)

        def finish(d):
            a0, l0, m0 = _softmax_parts(d, kv(v0))
            blk = pl.ds(n * QB, QB)
            s0 = m0 * EXP2_SCALE
            s1 = lse1[blk, :]
            s2 = lse2[blk, :]
            mx = jnp.maximum(jnp.maximum(s0, s1), s2)
            w0 = jnp.exp2(s0 - mx)
            w1 = jnp.exp2(s1 - mx)
            w2 = jnp.exp2(s2 - mx)
            num = w0 * a0 + w1 * out1[blk, :] + w2 * out2[blk, :]
            den = w0 * l0 + w1 + w2
            o_ref[blk, :] = (num / den * _silu(z_ref[blk, :].astype(F32))).astype(o_ref.dtype)

        return (lambda: _nt_dot(rows(q0, n * QB), kv(k0)) + bias[d0, n > 0]), finish

    tasks = [task1(r, n) for r in range(d1) for n in range(nq1)]
    tasks += [task2(r) for r in range(d2)] + [((lambda: None), second_hop)]
    tasks += [task0(n) for n in range(s_len // QB)]

    scores = [score() for score, _ in tasks[:SCORE_LOOKAHEAD]]
    for i, (_, finish) in enumerate(tasks):
        if i + SCORE_LOOKAHEAD < len(tasks):
            scores.append(tasks[i + SCORE_LOOKAHEAD][0]())
        finish(scores[i])


def _dilated_attention(proj, slopes, tile_rows):
    b, s, _ = proj.shape
    blk = (None, s, HEAD_DIM)

    def col(c):
        return pl.BlockSpec(blk, lambda bi, hi, sl: (bi, 0, c * N_HEADS + hi))

    grid_spec = pltpu.PrefetchScalarGridSpec(
        num_scalar_prefetch=1,
        grid=(b, N_HEADS),
        in_specs=[col(c) for c in range(3 * N_GROUPS + 1)],
        out_specs=pl.BlockSpec(blk, lambda bi, hi, sl: (bi, 0, hi)),
        scratch_shapes=[pltpu.VMEM((s, HEAD_DIM), F32) for _ in range(6)],
    )
    kern = functools.partial(_dilated_kernel, tile_rows=tile_rows)
    return pl.pallas_call(
        kern,
        grid_spec=grid_spec,
        out_shape=jax.ShapeDtypeStruct((b, s, N_HEADS * HEAD_DIM), BF16),
        compiler_params=_params("parallel", "arbitrary"),
        name="dilated_mixer",
    )(slopes, *([proj] * (3 * N_GROUPS + 1)))


N_BIAS_PARTS = 3
GATE_ROWS = 16
MOBA_HEADS = 2
MOBA_ROW_CHUNK = 64


def _moba_kernel(slopes_ref, q_ref, k_ref, v_ref, z_ref, o_ref, ka_ref, va_ref):
    s_len = q_ref.shape[0]
    blk = MOBA_BLOCK
    nblk = s_len // blk
    heads = range(MOBA_HEADS)
    lanes = [slice(hh * HEAD_DIM, (hh + 1) * HEAD_DIM) for hh in heads]

    @pl.when(pl.program_id(1) == 0)
    def _():
        pos = lax.broadcasted_iota(jnp.int32, (s_len, HEAD_DIM), 0)
        lane = lax.broadcasted_iota(jnp.int32, (s_len, HEAD_DIM), 1)
        for hh in heads:
            slope = slopes_ref[pl.program_id(0) * MOBA_HEADS + hh]
            bias = pos.astype(F32) * (slope / SM_SCALE)
            aug = jnp.where(lane == pos // blk, 1.0, 0.0)
            for part in range(N_BIAS_PARTS):
                piece = bias.astype(BF16).astype(F32)
                aug = jnp.where(lane == nblk + part, piece, aug)
                bias = bias - piece
            ka_ref[hh, :, HEAD_DIM:] = aug.astype(BF16)
            va_ref[hh, :, HEAD_DIM:] = jnp.ones((s_len, HEAD_DIM), BF16)

    row = lax.broadcasted_iota(jnp.int32, (GATE_ROWS, s_len), 0)
    col = lax.broadcasted_iota(jnp.int32, (GATE_ROWS, s_len), 1)
    ind = jnp.where(col // blk == row, 1.0 / blk, 0.0).astype(BF16)
    kmean = [jnp.dot(ind, k_ref[:, lanes[hh]], preferred_element_type=F32) for hh in heads]
    gate_t = []
    for hh in heads:
        ka_ref[hh, :, :HEAD_DIM] = k_ref[:, lanes[hh]]
        va_ref[hh, :, :HEAD_DIM] = v_ref[:, lanes[hh]]
        k_hi = kmean[hh].astype(BF16)
        k_lo = (kmean[hh] - k_hi.astype(F32)).astype(BF16)
        g2 = _nt_dot(jnp.concatenate([k_hi, k_lo], axis=0), q_ref[:, lanes[hh]])
        gate_t.append((g2[:GATE_ROWS] + g2[GATE_ROWS:])[:nblk, :])

    sub = lax.broadcasted_iota(jnp.int32, (nblk, blk), 0)
    tq = lax.broadcasted_iota(jnp.int32, (blk, blk), 0)
    sk = lax.broadcasted_iota(jnp.int32, (blk, blk), 1)
    causal = jnp.where(tq >= sk, 0.0, NEG)
    aug_row = lax.broadcasted_iota(jnp.int32, (HEAD_DIM, blk), 0)
    unit_rows = jnp.where((aug_row >= nblk) & (aug_row < nblk + N_BIAS_PARTS), 1.0, 0.0)
    pad_rows = jnp.zeros((HEAD_DIM - nblk, blk), F32)

    scores = {}
    for n in range(nblk):
        rows = pl.ds(n * blk, blk)
        for hh in heads:
            g = gate_t[hh][:, n * blk:(n + 1) * blk]
            rank = jnp.zeros((nblk, blk), jnp.int32)
            for mp in range(n):
                g_mp = g[mp:mp + 1, :]
                beats = (g_mp > g) | ((g_mp == g) & (mp < sub))
                rank = rank + jnp.where(beats, 1, 0)
            allowed = ((sub < n) & (rank < MOBA_TOPK)) | (sub == n)
            mask_t = jnp.concatenate([jnp.where(allowed, 0.0, NEG), pad_rows], axis=0) + unit_rows
            qa = jnp.concatenate([q_ref[rows, lanes[hh]], mask_t.T.astype(BF16)], axis=1)
            scores[hh, n] = _nt_dot(qa, ka_ref[hh, pl.ds(0, (n + 1) * blk), :])

    for n in range(nblk):
        rows = pl.ds(n * blk, blk)
        nk = (n + 1) * blk
        for hh in heads:
            d = scores[hh, n]
            parts = []
            for r0 in range(0, blk, MOBA_ROW_CHUNK):
                rc = slice(r0, r0 + MOBA_ROW_CHUNK)
                d_own = d[rc, n * blk:] + causal[rc]
                m = jnp.max(d_own, axis=-1, keepdims=True)
                if n > 0:
                    d_past = d[rc, :n * blk]
                    m = jnp.maximum(m, jnp.max(d_past, axis=-1, keepdims=True))
                    pc = jnp.concatenate([_exp_weights(d_past, m), _exp_weights(d_own, m)], axis=1)
                else:
                    pc = _exp_weights(d_own, m)
                parts.append(pc.astype(BF16))
            p = jnp.concatenate(parts, axis=0)
            acc = jnp.dot(p, va_ref[hh, pl.ds(0, nk), :], preferred_element_type=F32)
            o = acc[:, :HEAD_DIM] / acc[:, HEAD_DIM:]
            z = z_ref[rows, lanes[hh]].astype(F32)
            o_ref[rows, lanes[hh]] = (o * _silu(z)).astype(o_ref.dtype)


def _moba_attention(proj, slopes):
    b, s, _ = proj.shape
    blk = (None, s, MOBA_HEADS * HEAD_DIM)
    n_steps = N_HEADS // MOBA_HEADS

    def col(c):
        return pl.BlockSpec(blk, lambda hi, bi, sl: (bi, 0, c * n_steps + hi))

    grid_spec = pltpu.PrefetchScalarGridSpec(
        num_scalar_prefetch=1,
        grid=(n_steps, b),
        in_specs=[col(c) for c in range(4)],
        out_specs=pl.BlockSpec(blk, lambda hi, bi, sl: (bi, 0, hi)),
        scratch_shapes=[pltpu.VMEM((MOBA_HEADS, s, 2 * HEAD_DIM), BF16),
                        pltpu.VMEM((MOBA_HEADS, s, 2 * HEAD_DIM), BF16)],
    )
    return pl.pallas_call(
        _moba_kernel,
        grid_spec=grid_spec,
        out_shape=jax.ShapeDtypeStruct((b, s, N_HEADS * HEAD_DIM), BF16),
        compiler_params=_params("arbitrary", "arbitrary"),
        name="moba_mixer",
    )(slopes, proj, proj, proj, proj)


def kernel(x, c, norm_w, mod_w, mod_b, a_w_in, a_w_out, b_w_in, b_w_out, final_norm_w):
    b, s, d = x.shape
    depth = norm_w.shape[0]
    slopes = jnp.exp2(-8.0 * jnp.arange(1, N_HEADS + 1, dtype=F32) / N_HEADS)
    mod = _modulation(c, mod_w, mod_b)
    norm_w3 = norm_w.reshape(depth, 1, d)
    xf = x.reshape(b * s, d)
    for i in range(depth):
        j = i // 2
        final_w = final_norm_w if i == depth - 1 else None
        if i % 2 == 0:
            proj = _in_proj(_modulate(xf, norm_w3, mod, i, s, DILATIONS), a_w_in, i, j)
            og = _dilated_attention(proj.reshape(b, s, -1), slopes, MOD_TM)
            xf = _out_proj(og.reshape(b * s, -1), a_w_out, xf, mod, i, j, s, final_w)
        else:
            proj = _in_proj(_modulate(xf, norm_w3, mod, i, s, (1,)), b_w_in, i, j)
            og = _moba_attention(proj.reshape(b, s, -1), slopes)
            xf = _out_proj(og.reshape(b * s, -1), b_w_out, xf, mod, i, j, s, final_w)
    return xf.reshape(b, s, d)
```

```python
import math

import jax
import jax.numpy as jnp
from jax import lax
from jax.experimental import pallas as pl
from jax.experimental.pallas import tpu as pltpu

F32 = jnp.float32
BF16 = jnp.bfloat16

HEAD_DIM = 128
N_HEADS = 16
DSWA_PATTERNS = ((128, 1), (512, 4), (2048, 16))
DILATIONS = tuple(d for _, d in DSWA_PATTERNS)
N_GROUPS = len(DSWA_PATTERNS)
QB = 128
MOBA_BLOCK = 256
MOBA_TOPK = 3
EPS = 1e-6
NEG = -1e30
SM_SCALE = HEAD_DIM ** -0.5
EXP2_SCALE = SM_SCALE * math.log2(math.e)

LANES = 128
VMEM_LIMIT_BYTES = 56 * 1024 * 1024
MOD_PAD_ROWS = 16
PROJ_TM = 1024
PROJ_TN = 1024
MOD_TM = 512
MOD_ROW_CHUNK = 64
SCORE_LOOKAHEAD = 4


def _params(*sem):
    return pltpu.CompilerParams(dimension_semantics=sem, vmem_limit_bytes=VMEM_LIMIT_BYTES)


def _silu(z):
    return z / (1.0 + jnp.exp(-z))


def _mod_kernel(c_ref, w_ref, b_ref, o_ref):
    cond = _silu(c_ref[...])
    acc = jnp.dot(cond.astype(BF16), w_ref[...].astype(BF16), preferred_element_type=F32)
    o_ref[...] = acc + b_ref[...]


def _modulation(c, mod_w, mod_b):
    depth, d, n = mod_w.shape
    b = c.shape[0]
    tn = n // 4
    c_pad = jnp.pad(c, ((0, MOD_PAD_ROWS - b), (0, 0)))
    out = pl.pallas_call(
        _mod_kernel,
        grid=(depth, n // tn),
        in_specs=[
            pl.BlockSpec((MOD_PAD_ROWS, d), lambda l, j: (0, 0)),
            pl.BlockSpec((None, d, tn), lambda l, j: (l, 0, j)),
            pl.BlockSpec((None, 1, tn), lambda l, j: (l, 0, j)),
        ],
        out_specs=pl.BlockSpec((None, MOD_PAD_ROWS, tn), lambda l, j: (l, 0, j)),
        out_shape=jax.ShapeDtypeStruct((depth, MOD_PAD_ROWS, n), F32),
        compiler_params=_params("parallel", "arbitrary"),
        name="adaln_modulation",
    )(c_pad, mod_w, mod_b.reshape(depth, 1, n))
    return out[:, :b].reshape(depth, b, 3, d)


def _modulate_kernel(x_ref, nw_ref, mod_ref, h_ref, *stage, dilations):
    tm, d_model = x_ref.shape
    n_levels = len(dilations) - 1
    x = x_ref[...]
    rs = lax.rsqrt(jnp.mean(x * x, axis=-1, keepdims=True) + EPS)
    for c in range(d_model // LANES):
        cs = slice(c * LANES, (c + 1) * LANES)
        stages = [stage[0].at[g * 2 + c % 2] for g in range(n_levels)]
        for r0 in range(0, tm, MOD_ROW_CHUNK):
            rc = slice(r0, r0 + MOD_ROW_CHUNK)
            y = x_ref[rc, cs] * rs[rc] * nw_ref[:, cs]
            hc = y * (1.0 + mod_ref[1:2, cs]) + mod_ref[0:1, cs]
            h_ref[0, rc, cs] = hc.astype(BF16)
            if n_levels:
                stages[0][rc, :] = hc
        for g in range(1, n_levels + 1):
            src = stages[g - 1]
            prev, dil = dilations[g - 1], dilations[g]
            ratio, rows = dil // prev, tm // dil
            for rp in range(prev):
                for q in range(ratio):
                    slab = src[pl.ds(rp * (tm // prev) + q, rows, stride=ratio), :]
                    dst = pl.ds((rp + prev * q) * rows, rows)
                    h_ref[g, dst, cs] = slab.astype(BF16)
                    if g < n_levels:
                        stages[g][dst, :] = slab


def _modulate(xf, norm_w3, mod, layer, seq, dilations):
    m, d = xf.shape
    n_copies = len(dilations)
    tm = MOD_TM
    tiles_per_batch = seq // tm
    scratch = [pltpu.VMEM((2 * (n_copies - 1), tm, LANES), F32)] if n_copies > 1 else []
    kern = lambda *refs: _modulate_kernel(*refs, dilations=dilations)
    return pl.pallas_call(
        kern,
        grid=(m // tm,),
        in_specs=[
            pl.BlockSpec((tm, d), lambda i: (i, 0)),
            pl.BlockSpec((None, 1, d), lambda i: (layer, 0, 0)),
            pl.BlockSpec((None, None, 3, d), lambda i: (layer, i // tiles_per_batch, 0, 0)),
        ],
        out_specs=pl.BlockSpec((n_copies, tm, d), lambda i: (0, i, 0)),
        out_shape=jax.ShapeDtypeStruct((n_copies, m, d), BF16),
        scratch_shapes=scratch,
        compiler_params=_params("parallel"),
        name=f"modulate_l{layer}",
    )(xf, norm_w3, mod)


def _inproj_kernel(h_ref, w_ref, o_ref, wb_ref):
    @pl.when(pl.program_id(1) == 0)
    def _():
        wb_ref[...] = w_ref[...].astype(BF16)

    acc = jnp.dot(h_ref[...], wb_ref[...], preferred_element_type=F32)
    o_ref[...] = acc.astype(o_ref.dtype)


def _in_proj(h, w_in, layer, widx):
    n_copies, m, d = h.shape
    n = w_in.shape[-1]
    tm, tn = PROJ_TM, PROJ_TN
    tiles_per_copy = 3 * N_HEADS * HEAD_DIM // tn

    def copy_of(j):
        if n_copies == 1:
            return 0
        return jnp.where(j < tiles_per_copy * n_copies, j // tiles_per_copy, 0)

    return pl.pallas_call(
        _inproj_kernel,
        grid=(n // tn, m // tm),
        in_specs=[
            pl.BlockSpec((None, tm, d), lambda j, i: (copy_of(j), i, 0)),
            pl.BlockSpec((None, d, tn), lambda j, i: (widx, 0, j)),
        ],
        out_specs=pl.BlockSpec((tm, tn), lambda j, i: (i, j)),
        out_shape=jax.ShapeDtypeStruct((m, n), BF16),
        scratch_shapes=[pltpu.VMEM((d, tn), BF16)],
        compiler_params=_params("arbitrary", "arbitrary"),
        name=f"in_proj_l{layer}",
    )(h, w_in)


def _outproj_kernel(og_ref, w_ref, x_ref, mod_ref, *rest, final):
    if final:
        fw_ref, o_ref, wb_ref = rest
    else:
        o_ref, wb_ref = rest

    @pl.when(pl.program_id(1) == 0)
    def _():
        wb_ref[...] = w_ref[...].astype(BF16)

    y = jnp.dot(og_ref[...], wb_ref[...], preferred_element_type=F32)
    xn = x_ref[...] + mod_ref[2:3, :] * y
    if final:
        ms = jnp.mean(xn * xn, axis=-1, keepdims=True)
        xn = xn * lax.rsqrt(ms + EPS) * fw_ref[...]
    o_ref[...] = xn


def _out_proj(og, w_out, xf, mod, layer, widx, seq, final_w=None):
    m, k = og.shape
    n = w_out.shape[-1]
    final = final_w is not None
    tm, tn = (PROJ_TM // 2, n) if final else (PROJ_TM, PROJ_TN)
    tiles_per_batch = seq // tm
    w_mode = dict(pipeline_mode=pl.Buffered(1)) if final else {}
    in_specs = [
        pl.BlockSpec((tm, k), lambda j, i: (i, 0)),
        pl.BlockSpec((None, k, tn), lambda j, i: (widx, 0, j), **w_mode),
        pl.BlockSpec((tm, tn), lambda j, i: (i, j)),
        pl.BlockSpec((None, None, 3, tn), lambda j, i: (layer, i // tiles_per_batch, 0, j)),
    ]
    args = [og, w_out, xf, mod]
    if final:
        in_specs.append(pl.BlockSpec((1, n), lambda j, i: (0, 0)))
        args.append(final_w.reshape(1, n))
    kern = lambda *refs: _outproj_kernel(*refs, final=final)
    return pl.pallas_call(
        kern,
        grid=(n // tn, m // tm),
        in_specs=in_specs,
        out_specs=pl.BlockSpec((tm, tn), lambda j, i: (i, j)),
        out_shape=jax.ShapeDtypeStruct((m, n), F32),
        scratch_shapes=[pltpu.VMEM((k, tn), BF16)],
        compiler_params=_params("arbitrary", "arbitrary"),
        name=f"out_proj_l{layer}",
    )(*args)


def _nt_dot(a, b):
    return lax.dot_general(a, b, (((1,), (1,)), ((), ())), preferred_element_type=F32)


def _exp_weights(d, m):
    return jnp.exp2((d - m) * EXP2_SCALE)


def _pv_with_denominator(p, v):
    v1 = jnp.concatenate([v, jnp.ones(v.shape, v.dtype)], axis=1)
    acc = jnp.dot(p.astype(BF16), v1, preferred_element_type=F32)
    return acc[:, :HEAD_DIM], acc[:, HEAD_DIM:]


def _band_bias(slope, dilation, with_prev):
    nk = 2 * QB if with_prev else QB
    qi = lax.broadcasted_iota(jnp.int32, (QB, nk), 0)
    ki = lax.broadcasted_iota(jnp.int32, (QB, nk), 1)
    step = qi - ki + (QB if with_prev else 0)
    ok = (step >= 0) & (step <= QB)
    return jnp.where(ok, (-slope * dilation / SM_SCALE) * step.astype(F32), NEG)


def _softmax_parts(d, v):
    m = jnp.max(d, axis=-1, keepdims=True)
    acc, l = _pv_with_denominator(_exp_weights(d, m), v)
    return acc, l, jnp.broadcast_to(m, acc.shape)


def _softmax_out_lse(d, v):
    acc, l, m = _softmax_parts(d, v)
    return acc / l, m * EXP2_SCALE + jnp.log2(l)


def _dilated_kernel(slopes_ref, q0, k0, v0, q1, k1, v1, q2, k2, v2, z_ref, o_ref,
                    out1, lse1, out2, lse2, tout2, tlse2, *, tile_rows):
    slope = slopes_ref[pl.program_id(1)]
    s_len = q0.shape[0]
    d0, d1, d2 = DILATIONS

    def rows(ref, start, size=QB):
        return ref[pl.ds(start, size), :]

    def prev_cur(ref, start_prev, start_cur):
        if start_cur == start_prev + QB:
            return rows(ref, start_prev, 2 * QB)
        return jnp.concatenate([rows(ref, start_prev), rows(ref, start_cur)], axis=0)

    per_res1 = tile_rows // d1
    per_res2 = tile_rows // d2
    n_tiles = s_len // tile_rows
    nq1 = s_len // d1 // QB

    def start1(r, n):
        l0 = n * QB
        return (l0 // per_res1) * tile_rows + r * per_res1 + l0 % per_res1

    def gather2(ref, r):
        return jnp.concatenate([rows(ref, t * tile_rows + r * per_res2, per_res2)
                                for t in range(n_tiles)], axis=0)

    bias = {(d, wp): _band_bias(slope, d, wp) for d in DILATIONS for wp in (False, True)}

    def task1(r, n):
        def kv(ref):
            return rows(ref, start1(r, 0)) if n == 0 else prev_cur(ref, start1(r, n - 1), start1(r, n))

        def finish(d):
            dst = pl.ds(n * QB * d1 + r, QB, stride=d1)
            out1[dst, :], lse1[dst, :] = _softmax_out_lse(d, kv(v1))

        return (lambda: _nt_dot(rows(q1, start1(r, n)), kv(k1)) + bias[d1, n > 0]), finish

    def task2(r):
        def finish(d):
            dst = pl.ds((r % d1) * (s_len // d1) + r // d1, QB, stride=d2 // d1)
            tout2[dst, :], tlse2[dst, :] = _softmax_out_lse(d, gather2(v2, r))

        return (lambda: _nt_dot(gather2(q2, r), gather2(k2, r)) + bias[d2, False]), finish

    def second_hop(_):
        for r in range(d1):
            src = pl.ds(r * (s_len // d1), s_len // d1)
            dst = pl.ds(r, s_len // d1, stride=d1)
            out2[dst, :] = tout2[src, :]
            lse2[dst, :] = tlse2[src, :]

    def task0(n):
        def kv(ref):
            return rows(ref, 0) if n == 0 else rows(ref, (n - 1) * QB, 2 * QB)

        def finish(d):
            a0, l0, m0 = _softmax_parts(d, kv(v0))
            blk = pl.ds(n * QB, QB)
            s0 = m0 * EXP2_SCALE
            s1 = lse1[blk, :]
            s2 = lse2[blk, :]
            mx = jnp.maximum(jnp.maximum(s0, s1), s2)
            w0 = jnp.exp2(s0 - mx)
            w1 = jnp.exp2(s1 - mx)
            w2 = jnp.exp2(s2 - mx)
            num = w0 * a0 + w1 * out1[blk, :] + w2 * out2[blk, :]
            den = w0 * l0 + w1 + w2
            o_ref[blk, :] = (num / den * _silu(z_ref[blk, :].astype(F32))).astype(o_ref.dtype)

        return (lambda: _nt_dot(rows(q0, n * QB), kv(k0)) + bias[d0, n > 0]), finish

    tasks = [task1(r, n) for r in range(d1) for n in range(nq1)]
    tasks += [task2(r) for r in range(d2)] + [((lambda: None), second_hop)]
    tasks += [task0(n) for n in range(s_len // QB)]

    scores = [score() for score, _ in tasks[:SCORE_LOOKAHEAD]]
    for i, (_, finish) in enumerate(tasks):
        if i + SCORE_LOOKAHEAD < len(tasks):
            scores.append(tasks[i + SCORE_LOOKAHEAD][0]())
        finish(scores[i])


def _dilated_attention(proj, slopes, tile_rows):
    b, s, _ = proj.shape
    blk = (None, s, HEAD_DIM)

    def col(c):
        return pl.BlockSpec(blk, lambda bi, hi, sl: (bi, 0, c * N_HEADS + hi))

    grid_spec = pltpu.PrefetchScalarGridSpec(
        num_scalar_prefetch=1,
        grid=(b, N_HEADS),
        in_specs=[col(c) for c in range(3 * N_GROUPS + 1)],
        out_specs=pl.BlockSpec(blk, lambda bi, hi, sl: (bi, 0, hi)),
        scratch_shapes=[pltpu.VMEM((s, HEAD_DIM), F32) for _ in range(6)],
    )
    kern = lambda *refs: _dilated_kernel(*refs, tile_rows=tile_rows)
    return pl.pallas_call(
        kern,
        grid_spec=grid_spec,
        out_shape=jax.ShapeDtypeStruct((b, s, N_HEADS * HEAD_DIM), BF16),
        compiler_params=_params("parallel", "arbitrary"),
        name="dilated_mixer",
    )(slopes, *([proj] * (3 * N_GROUPS + 1)))


N_BIAS_PARTS = 3
GATE_ROWS = 16
MOBA_HEADS = 2
MOBA_ROW_CHUNK = 64


def _moba_key_columns(slopes, s_len):
    nblk = s_len // MOBA_BLOCK
    pos = jnp.arange(s_len, dtype=jnp.int32)[None, :, None]
    lane = jnp.arange(HEAD_DIM, dtype=jnp.int32)[None, None, :]
    bias = pos.astype(F32) * (slopes[:, None, None] / SM_SCALE)
    cols = jnp.broadcast_to(jnp.where(lane == pos // MOBA_BLOCK, 1.0, 0.0), (slopes.shape[0], s_len, HEAD_DIM))
    for part in range(N_BIAS_PARTS):
        piece = lax.reduce_precision(bias, exponent_bits=8, mantissa_bits=7)
        cols = jnp.where(lane == nblk + part, piece, cols)
        bias = bias - piece
    return cols.astype(BF16)


def _moba_kernel(q_ref, k_ref, v_ref, z_ref, kcol_ref, o_ref, ka_ref, va_ref):
    s_len = q_ref.shape[0]
    blk = MOBA_BLOCK
    nblk = s_len // blk
    heads = range(MOBA_HEADS)
    lanes = [slice(hh * HEAD_DIM, (hh + 1) * HEAD_DIM) for hh in heads]

    @pl.when(pl.program_id(1) == 0)
    def _():
        for hh in heads:
            ka_ref[hh, :, HEAD_DIM:] = kcol_ref[hh]
            va_ref[hh, :, HEAD_DIM:] = jnp.ones((s_len, HEAD_DIM), BF16)

    row = lax.broadcasted_iota(jnp.int32, (GATE_ROWS, s_len), 0)
    col = lax.broadcasted_iota(jnp.int32, (GATE_ROWS, s_len), 1)
    ind = jnp.where(col // blk == row, 1.0 / blk, 0.0).astype(BF16)
    kmean = [jnp.dot(ind, k_ref[:, lanes[hh]], preferred_element_type=F32) for hh in heads]
    gate_t = []
    for hh in heads:
        ka_ref[hh, :, :HEAD_DIM] = k_ref[:, lanes[hh]]
        va_ref[hh, :, :HEAD_DIM] = v_ref[:, lanes[hh]]
        k_hi = kmean[hh].astype(BF16)
        k_lo = (kmean[hh] - k_hi.astype(F32)).astype(BF16)
        g2 = _nt_dot(jnp.concatenate([k_hi, k_lo], axis=0), q_ref[:, lanes[hh]])
        gate_t.append((g2[:GATE_ROWS] + g2[GATE_ROWS:])[:nblk, :])

    sub = lax.broadcasted_iota(jnp.int32, (nblk, blk), 0)
    tq = lax.broadcasted_iota(jnp.int32, (blk, blk), 0)
    sk = lax.broadcasted_iota(jnp.int32, (blk, blk), 1)
    causal = jnp.where(tq >= sk, 0.0, NEG)
    aug_row = lax.broadcasted_iota(jnp.int32, (HEAD_DIM, blk), 0)
    unit_rows = jnp.where((aug_row >= nblk) & (aug_row < nblk + N_BIAS_PARTS), 1.0, 0.0)
    pad_rows = jnp.zeros((HEAD_DIM - nblk, blk), F32)

    scores = {}
    for n in range(nblk):
        rows = pl.ds(n * blk, blk)
        for hh in heads:
            g = gate_t[hh][:, n * blk:(n + 1) * blk]
            rank = jnp.zeros((nblk, blk), jnp.int32)
            for mp in range(n):
                g_mp = g[mp:mp + 1, :]
                beats = (g_mp > g) | ((g_mp == g) & (mp < sub))
                rank = rank + jnp.where(beats, 1, 0)
            allowed = ((sub < n) & (rank < MOBA_TOPK)) | (sub == n)
            mask_t = jnp.concatenate([jnp.where(allowed, 0.0, NEG), pad_rows], axis=0) + unit_rows
            qa = jnp.concatenate([q_ref[rows, lanes[hh]], mask_t.T.astype(BF16)], axis=1)
            scores[hh, n] = _nt_dot(qa, ka_ref[hh, pl.ds(0, (n + 1) * blk), :])

    for n in range(nblk):
        rows = pl.ds(n * blk, blk)
        nk = (n + 1) * blk
        for hh in heads:
            d = scores[hh, n]
            parts = []
            for r0 in range(0, blk, MOBA_ROW_CHUNK):
                rc = slice(r0, r0 + MOBA_ROW_CHUNK)
                d_own = d[rc, n * blk:] + causal[rc]
                m = jnp.max(d_own, axis=-1, keepdims=True)
                if n > 0:
                    d_past = d[rc, :n * blk]
                    m = jnp.maximum(m, jnp.max(d_past, axis=-1, keepdims=True))
                    pc = jnp.concatenate([_exp_weights(d_past, m), _exp_weights(d_own, m)], axis=1)
                else:
                    pc = _exp_weights(d_own, m)
                parts.append(pc.astype(BF16))
            p = jnp.concatenate(parts, axis=0)
            acc = jnp.dot(p, va_ref[hh, pl.ds(0, nk), :], preferred_element_type=F32)
            o = acc[:, :HEAD_DIM] / acc[:, HEAD_DIM:]
            z = z_ref[rows, lanes[hh]].astype(F32)
            o_ref[rows, lanes[hh]] = (o * _silu(z)).astype(o_ref.dtype)


def _moba_attention(proj, key_cols):
    b, s, _ = proj.shape
    blk = (None, s, MOBA_HEADS * HEAD_DIM)
    n_steps = N_HEADS // MOBA_HEADS

    def col(c):
        return pl.BlockSpec(blk, lambda hi, bi: (bi, 0, c * n_steps + hi))

    return pl.pallas_call(
        _moba_kernel,
        grid=(n_steps, b),
        in_specs=[col(c) for c in range(4)] + [
            pl.BlockSpec((MOBA_HEADS, s, HEAD_DIM), lambda hi, bi: (hi, 0, 0))],
        out_specs=pl.BlockSpec(blk, lambda hi, bi: (bi, 0, hi)),
        out_shape=jax.ShapeDtypeStruct((b, s, N_HEADS * HEAD_DIM), BF16),
        scratch_shapes=[pltpu.VMEM((MOBA_HEADS, s, 2 * HEAD_DIM), BF16),
                        pltpu.VMEM((MOBA_HEADS, s, 2 * HEAD_DIM), BF16)],
        compiler_params=_params("arbitrary", "arbitrary"),
        name="moba_mixer",
    )(proj, proj, proj, proj, key_cols)


def kernel(x, c, norm_w, mod_w, mod_b, a_w_in, a_w_out, b_w_in, b_w_out, final_norm_w):
    b, s, d = x.shape
    depth = norm_w.shape[0]
    slopes = jnp.exp2(-8.0 * jnp.arange(1, N_HEADS + 1, dtype=F32) / N_HEADS)
    key_cols = _moba_key_columns(slopes, s)
    mod = _modulation(c, mod_w, mod_b)
    norm_w3 = norm_w.reshape(depth, 1, d)
    xf = x.reshape(b * s, d)
    for i in range(depth):
        j = i // 2
        final_w = final_norm_w if i == depth - 1 else None
        if i % 2 == 0:
            proj = _in_proj(_modulate(xf, norm_w3, mod, i, s, DILATIONS), a_w_in, i, j)
            og = _dilated_attention(proj.reshape(b, s, -1), slopes, MOD_TM)
            xf = _out_proj(og.reshape(b * s, -1), a_w_out, xf, mod, i, j, s, final_w)
        else:
            proj = _in_proj(_modulate(xf, norm_w3, mod, i, s, (1,)), b_w_in, i, j)
            og = _moba_attention(proj.reshape(b, s, -1), key_cols)
            xf = _out_proj(og.reshape(b * s, -1), b_w_out, xf, mod, i, j, s, final_w)
    return xf.reshape(b, s, d)
```

```python
import math

import jax
import jax.numpy as jnp
import numpy as np
from jax import lax
from jax.experimental import pallas as pl
from jax.experimental.pallas import tpu as pltpu

F32 = jnp.float32
BF16 = jnp.bfloat16

HEAD_DIM = 128
N_HEADS = 16
DSWA_PATTERNS = ((128, 1), (512, 4), (2048, 16))
DILATIONS = tuple(d for _, d in DSWA_PATTERNS)
N_GROUPS = len(DSWA_PATTERNS)
QB = 128
MOBA_BLOCK = 256
MOBA_TOPK = 3
EPS = 1e-6
NEG = -1e30
SM_SCALE = HEAD_DIM ** -0.5
EXP2_SCALE = SM_SCALE * math.log2(math.e)

LANES = 128
VMEM_LIMIT_BYTES = 56 * 1024 * 1024
MOD_PAD_ROWS = 16
PROJ_TM = 1024
PROJ_TN = 1024
MOD_TM = 512
MOD_ROW_CHUNK = 64
SCORE_LOOKAHEAD = 4


def _params(*sem):
    return pltpu.CompilerParams(dimension_semantics=sem, vmem_limit_bytes=VMEM_LIMIT_BYTES)


def _silu(z):
    return z / (1.0 + jnp.exp(-z))


def _mod_kernel(c_ref, w_ref, b_ref, o_ref):
    cond = _silu(c_ref[...])
    acc = jnp.dot(cond.astype(BF16), w_ref[...].astype(BF16), preferred_element_type=F32)
    o_ref[...] = acc + b_ref[...]


def _modulation(c, mod_w, mod_b):
    depth, d, n = mod_w.shape
    b = c.shape[0]
    tn = n // 4
    c_pad = jnp.pad(c, ((0, MOD_PAD_ROWS - b), (0, 0)))
    out = pl.pallas_call(
        _mod_kernel,
        grid=(depth, n // tn),
        in_specs=[
            pl.BlockSpec((MOD_PAD_ROWS, d), lambda l, j: (0, 0)),
            pl.BlockSpec((None, d, tn), lambda l, j: (l, 0, j)),
            pl.BlockSpec((None, 1, tn), lambda l, j: (l, 0, j)),
        ],
        out_specs=pl.BlockSpec((None, MOD_PAD_ROWS, tn), lambda l, j: (l, 0, j)),
        out_shape=jax.ShapeDtypeStruct((depth, MOD_PAD_ROWS, n), F32),
        compiler_params=_params("parallel", "arbitrary"),
        name="adaln_modulation",
    )(c_pad, mod_w, mod_b.reshape(depth, 1, n))
    return out[:, :b].reshape(depth, b, 3, d)


def _modulate_kernel(x_ref, nw_ref, mod_ref, h_ref, *stage, dilations):
    tm, d_model = x_ref.shape
    n_levels = len(dilations) - 1
    x = x_ref[...]
    rs = lax.rsqrt(jnp.mean(x * x, axis=-1, keepdims=True) + EPS)
    for c in range(d_model // LANES):
        cs = slice(c * LANES, (c + 1) * LANES)
        stages = [stage[0].at[g * 2 + c % 2] for g in range(n_levels)]
        for r0 in range(0, tm, MOD_ROW_CHUNK):
            rc = slice(r0, r0 + MOD_ROW_CHUNK)
            y = x_ref[rc, cs] * rs[rc] * nw_ref[:, cs]
            hc = y * (1.0 + mod_ref[1:2, cs]) + mod_ref[0:1, cs]
            h_ref[0, rc, cs] = hc.astype(BF16)
            if n_levels:
                stages[0][rc, :] = hc
        for g in range(1, n_levels + 1):
            src = stages[g - 1]
            prev, dil = dilations[g - 1], dilations[g]
            ratio, rows = dil // prev, tm // dil
            for rp in range(prev):
                for q in range(ratio):
                    slab = src[pl.ds(rp * (tm // prev) + q, rows, stride=ratio), :]
                    dst = pl.ds((rp + prev * q) * rows, rows)
                    h_ref[g, dst, cs] = slab.astype(BF16)
                    if g < n_levels:
                        stages[g][dst, :] = slab


def _modulate(xf, norm_w3, mod, layer, seq, dilations):
    m, d = xf.shape
    n_copies = len(dilations)
    tm = MOD_TM
    tiles_per_batch = seq // tm
    scratch = [pltpu.VMEM((2 * (n_copies - 1), tm, LANES), F32)] if n_copies > 1 else []
    kern = lambda *refs: _modulate_kernel(*refs, dilations=dilations)
    return pl.pallas_call(
        kern,
        grid=(m // tm,),
        in_specs=[
            pl.BlockSpec((tm, d), lambda i: (i, 0)),
            pl.BlockSpec((None, 1, d), lambda i: (layer, 0, 0)),
            pl.BlockSpec((None, None, 3, d), lambda i: (layer, i // tiles_per_batch, 0, 0)),
        ],
        out_specs=pl.BlockSpec((n_copies, tm, d), lambda i: (0, i, 0)),
        out_shape=jax.ShapeDtypeStruct((n_copies, m, d), BF16),
        scratch_shapes=scratch,
        compiler_params=_params("parallel"),
        name=f"modulate_l{layer}",
    )(xf, norm_w3, mod)


def _inproj_kernel(h_ref, w_ref, o_ref, wb_ref):
    @pl.when(pl.program_id(1) == 0)
    def _():
        wb_ref[...] = w_ref[...].astype(BF16)

    acc = jnp.dot(h_ref[...], wb_ref[...], preferred_element_type=F32)
    o_ref[...] = acc.astype(o_ref.dtype)


def _in_proj(h, w_in, layer, widx):
    n_copies, m, d = h.shape
    n = w_in.shape[-1]
    tm, tn = PROJ_TM, PROJ_TN
    tiles_per_copy = 3 * N_HEADS * HEAD_DIM // tn

    def copy_of(j):
        if n_copies == 1:
            return 0
        return jnp.where(j < tiles_per_copy * n_copies, j // tiles_per_copy, 0)

    return pl.pallas_call(
        _inproj_kernel,
        grid=(n // tn, m // tm),
        in_specs=[
            pl.BlockSpec((None, tm, d), lambda j, i: (copy_of(j), i, 0)),
            pl.BlockSpec((None, d, tn), lambda j, i: (widx, 0, j)),
        ],
        out_specs=pl.BlockSpec((tm, tn), lambda j, i: (i, j)),
        out_shape=jax.ShapeDtypeStruct((m, n), BF16),
        scratch_shapes=[pltpu.VMEM((d, tn), BF16)],
        compiler_params=_params("arbitrary", "arbitrary"),
        name=f"in_proj_l{layer}",
    )(h, w_in)


def _outproj_kernel(og_ref, w_ref, x_ref, mod_ref, *rest, final):
    if final:
        fw_ref, o_ref, wb_ref = rest
    else:
        o_ref, wb_ref = rest

    @pl.when(pl.program_id(1) == 0)
    def _():
        wb_ref[...] = w_ref[...].astype(BF16)

    y = jnp.dot(og_ref[...], wb_ref[...], preferred_element_type=F32)
    xn = x_ref[...] + mod_ref[2:3, :] * y
    if final:
        ms = jnp.mean(xn * xn, axis=-1, keepdims=True)
        xn = xn * lax.rsqrt(ms + EPS) * fw_ref[...]
    o_ref[...] = xn


def _out_proj(og, w_out, xf, mod, layer, widx, seq, final_w=None):
    m, k = og.shape
    n = w_out.shape[-1]
    final = final_w is not None
    tm, tn = (PROJ_TM // 2, n) if final else (PROJ_TM, PROJ_TN)
    tiles_per_batch = seq // tm
    w_mode = dict(pipeline_mode=pl.Buffered(1)) if final else {}
    in_specs = [
        pl.BlockSpec((tm, k), lambda j, i: (i, 0)),
        pl.BlockSpec((None, k, tn), lambda j, i: (widx, 0, j), **w_mode),
        pl.BlockSpec((tm, tn), lambda j, i: (i, j)),
        pl.BlockSpec((None, None, 3, tn), lambda j, i: (layer, i // tiles_per_batch, 0, j)),
    ]
    args = [og, w_out, xf, mod]
    if final:
        in_specs.append(pl.BlockSpec((1, n), lambda j, i: (0, 0)))
        args.append(final_w.reshape(1, n))
    kern = lambda *refs: _outproj_kernel(*refs, final=final)
    return pl.pallas_call(
        kern,
        grid=(n // tn, m // tm),
        in_specs=in_specs,
        out_specs=pl.BlockSpec((tm, tn), lambda j, i: (i, j)),
        out_shape=jax.ShapeDtypeStruct((m, n), F32),
        scratch_shapes=[pltpu.VMEM((k, tn), BF16)],
        compiler_params=_params("arbitrary", "arbitrary"),
        name=f"out_proj_l{layer}",
    )(*args)


def _nt_dot(a, b):
    return lax.dot_general(a, b, (((1,), (1,)), ((), ())), preferred_element_type=F32)


def _exp_weights(d, m):
    return jnp.exp2((d - m) * EXP2_SCALE)


def _pv_with_denominator(p, v):
    v1 = jnp.concatenate([v, jnp.ones(v.shape, v.dtype)], axis=1)
    acc = jnp.dot(p.astype(BF16), v1, preferred_element_type=F32)
    return acc[:, :HEAD_DIM], acc[:, HEAD_DIM:]


def _band_bias(slope, dilation, with_prev):
    nk = 2 * QB if with_prev else QB
    qi = lax.broadcasted_iota(jnp.int32, (QB, nk), 0)
    ki = lax.broadcasted_iota(jnp.int32, (QB, nk), 1)
    step = qi - ki + (QB if with_prev else 0)
    ok = (step >= 0) & (step <= QB)
    return jnp.where(ok, (-slope * dilation / SM_SCALE) * step.astype(F32), NEG)


def _softmax_parts(d, v):
    m = jnp.max(d, axis=-1, keepdims=True)
    acc, l = _pv_with_denominator(_exp_weights(d, m), v)
    return acc, l, jnp.broadcast_to(m, acc.shape)


def _softmax_out_lse(d, v):
    acc, l, m = _softmax_parts(d, v)
    return acc / l, m * EXP2_SCALE + jnp.log2(l)


def _dilated_kernel(slopes_ref, q0, k0, v0, q1, k1, v1, q2, k2, v2, z_ref, o_ref,
                    out1, lse1, out2, lse2, tout2, tlse2, *, tile_rows):
    slope = slopes_ref[pl.program_id(1)]
    s_len = q0.shape[0]
    d0, d1, d2 = DILATIONS

    def rows(ref, start, size=QB):
        return ref[pl.ds(start, size), :]

    def prev_cur(ref, start_prev, start_cur):
        if start_cur == start_prev + QB:
            return rows(ref, start_prev, 2 * QB)
        return jnp.concatenate([rows(ref, start_prev), rows(ref, start_cur)], axis=0)

    per_res1 = tile_rows // d1
    per_res2 = tile_rows // d2
    n_tiles = s_len // tile_rows
    nq1 = s_len // d1 // QB

    def start1(r, n):
        l0 = n * QB
        return (l0 // per_res1) * tile_rows + r * per_res1 + l0 % per_res1

    def gather2(ref, r):
        return jnp.concatenate([rows(ref, t * tile_rows + r * per_res2, per_res2)
                                for t in range(n_tiles)], axis=0)

    bias = {(d, wp): _band_bias(slope, d, wp) for d in DILATIONS for wp in (False, True)}

    def task1(r, n):
        def kv(ref):
            return rows(ref, start1(r, 0)) if n == 0 else prev_cur(ref, start1(r, n - 1), start1(r, n))

        def finish(d):
            dst = pl.ds(n * QB * d1 + r, QB, stride=d1)
            out1[dst, :], lse1[dst, :] = _softmax_out_lse(d, kv(v1))

        return (lambda: _nt_dot(rows(q1, start1(r, n)), kv(k1)) + bias[d1, n > 0]), finish

    def task2(r):
        def finish(d):
            dst = pl.ds((r % d1) * (s_len // d1) + r // d1, QB, stride=d2 // d1)
            tout2[dst, :], tlse2[dst, :] = _softmax_out_lse(d, gather2(v2, r))

        return (lambda: _nt_dot(gather2(q2, r), gather2(k2, r)) + bias[d2, False]), finish

    def second_hop(_):
        for r in range(d1):
            src = pl.ds(r * (s_len // d1), s_len // d1)
            dst = pl.ds(r, s_len // d1, stride=d1)
            out2[dst, :] = tout2[src, :]
            lse2[dst, :] = tlse2[src, :]

    def task0(n):
        def kv(ref):
            return rows(ref, 0) if n == 0 else rows(ref, (n - 1) * QB, 2 * QB)

        def finish(d):
            a0, l0, m0 = _softmax_parts(d, kv(v0))
            blk = pl.ds(n * QB, QB)
            s0 = m0 * EXP2_SCALE
            s1 = lse1[blk, :]
            s2 = lse2[blk, :]
            mx = jnp.maximum(jnp.maximum(s0, s1), s2)
            w0 = jnp.exp2(s0 - mx)
            w1 = jnp.exp2(s1 - mx)
            w2 = jnp.exp2(s2 - mx)
            num = w0 * a0 + w1 * out1[blk, :] + w2 * out2[blk, :]
            den = w0 * l0 + w1 + w2
            o_ref[blk, :] = (num / den * _silu(z_ref[blk, :].astype(F32))).astype(o_ref.dtype)

        return (lambda: _nt_dot(rows(q0, n * QB), kv(k0)) + bias[d0, n > 0]), finish

    tasks = [task1(r, n) for r in range(d1) for n in range(nq1)]
    tasks += [task2(r) for r in range(d2)] + [((lambda: None), second_hop)]
    tasks += [task0(n) for n in range(s_len // QB)]

    scores = [score() for score, _ in tasks[:SCORE_LOOKAHEAD]]
    for i, (_, finish) in enumerate(tasks):
        if i + SCORE_LOOKAHEAD < len(tasks):
            scores.append(tasks[i + SCORE_LOOKAHEAD][0]())
        finish(scores[i])


def _dilated_attention(proj, slopes, tile_rows):
    b, s, _ = proj.shape
    blk = (None, s, HEAD_DIM)

    def col(c):
        return pl.BlockSpec(blk, lambda bi, hi, sl: (bi, 0, c * N_HEADS + hi))

    grid_spec = pltpu.PrefetchScalarGridSpec(
        num_scalar_prefetch=1,
        grid=(b, N_HEADS),
        in_specs=[col(c) for c in range(3 * N_GROUPS + 1)],
        out_specs=pl.BlockSpec(blk, lambda bi, hi, sl: (bi, 0, hi)),
        scratch_shapes=[pltpu.VMEM((s, HEAD_DIM), F32) for _ in range(6)],
    )
    kern = lambda *refs: _dilated_kernel(*refs, tile_rows=tile_rows)
    return pl.pallas_call(
        kern,
        grid_spec=grid_spec,
        out_shape=jax.ShapeDtypeStruct((b, s, N_HEADS * HEAD_DIM), BF16),
        compiler_params=_params("parallel", "arbitrary"),
        name="dilated_mixer",
    )(slopes, *([proj] * (3 * N_GROUPS + 1)))


N_BIAS_PARTS = 3
GATE_ROWS = 16
MOBA_HEADS = 2
MOBA_ROW_CHUNK = 64


def _moba_key_columns(n_heads, s_len):
    nblk = s_len // MOBA_BLOCK
    slopes = np.exp2(-8.0 * np.arange(1, n_heads + 1, dtype=np.float32) / n_heads).astype(np.float32)
    pos = np.arange(s_len, dtype=np.int32)[None, :, None]
    lane = np.arange(HEAD_DIM, dtype=np.int32)[None, None, :]
    bias = (pos.astype(np.float32) * (slopes[:, None, None] / np.float32(SM_SCALE))).astype(np.float32)
    cols = np.broadcast_to((lane == pos // MOBA_BLOCK).astype(np.float32), (n_heads, s_len, HEAD_DIM)).copy()
    for part in range(N_BIAS_PARTS):
        piece = bias.astype(BF16).astype(np.float32)
        cols = np.where(lane == nblk + part, piece, cols)
        bias = bias - piece
    return jnp.asarray(cols.astype(BF16))


def _moba_kernel(q_ref, k_ref, v_ref, z_ref, kcol_ref, o_ref, ka_ref, va_ref):
    s_len = q_ref.shape[0]
    blk = MOBA_BLOCK
    nblk = s_len // blk
    heads = range(MOBA_HEADS)
    lanes = [slice(hh * HEAD_DIM, (hh + 1) * HEAD_DIM) for hh in heads]

    @pl.when(pl.program_id(1) == 0)
    def _():
        for hh in heads:
            ka_ref[hh, :, HEAD_DIM:] = kcol_ref[hh]
            va_ref[hh, :, HEAD_DIM:] = jnp.ones((s_len, HEAD_DIM), BF16)

    row = lax.broadcasted_iota(jnp.int32, (GATE_ROWS, s_len), 0)
    col = lax.broadcasted_iota(jnp.int32, (GATE_ROWS, s_len), 1)
    ind = jnp.where(col // blk == row, 1.0 / blk, 0.0).astype(BF16)
    kmean = [jnp.dot(ind, k_ref[:, lanes[hh]], preferred_element_type=F32) for hh in heads]
    gate_t = []
    for hh in heads:
        ka_ref[hh, :, :HEAD_DIM] = k_ref[:, lanes[hh]]
        va_ref[hh, :, :HEAD_DIM] = v_ref[:, lanes[hh]]
        k_hi = kmean[hh].astype(BF16)
        k_lo = (kmean[hh] - k_hi.astype(F32)).astype(BF16)
        g2 = _nt_dot(jnp.concatenate([k_hi, k_lo], axis=0), q_ref[:, lanes[hh]])
        gate_t.append((g2[:GATE_ROWS] + g2[GATE_ROWS:])[:nblk, :])

    sub = lax.broadcasted_iota(jnp.int32, (nblk, blk), 0)
    tq = lax.broadcasted_iota(jnp.int32, (blk, blk), 0)
    sk = lax.broadcasted_iota(jnp.int32, (blk, blk), 1)
    causal = jnp.where(tq >= sk, 0.0, NEG)
    aug_row = lax.broadcasted_iota(jnp.int32, (HEAD_DIM, blk), 0)
    unit_rows = jnp.where((aug_row >= nblk) & (aug_row < nblk + N_BIAS_PARTS), 1.0, 0.0)
    pad_rows = jnp.zeros((HEAD_DIM - nblk, blk), F32)

    scores = {}
    for n in range(nblk):
        rows = pl.ds(n * blk, blk)
        for hh in heads:
            g = gate_t[hh][:, n * blk:(n + 1) * blk]
            rank = jnp.zeros((nblk, blk), jnp.int32)
            for mp in range(n):
                g_mp = g[mp:mp + 1, :]
                beats = (g_mp > g) | ((g_mp == g) & (mp < sub))
                rank = rank + jnp.where(beats, 1, 0)
            allowed = ((sub < n) & (rank < MOBA_TOPK)) | (sub == n)
            mask_t = jnp.concatenate([jnp.where(allowed, 0.0, NEG), pad_rows], axis=0) + unit_rows
            qa = jnp.concatenate([q_ref[rows, lanes[hh]], mask_t.T.astype(BF16)], axis=1)
            scores[hh, n] = _nt_dot(qa, ka_ref[hh, pl.ds(0, (n + 1) * blk), :])

    for n in range(nblk):
        rows = pl.ds(n * blk, blk)
        nk = (n + 1) * blk
        for hh in heads:
            d = scores[hh, n]
            parts = []
            for r0 in range(0, blk, MOBA_ROW_CHUNK):
                rc = slice(r0, r0 + MOBA_ROW_CHUNK)
                d_own = d[rc, n * blk:] + causal[rc]
                m = jnp.max(d_own, axis=-1, keepdims=True)
                if n > 0:
                    d_past = d[rc, :n * blk]
                    m = jnp.maximum(m, jnp.max(d_past, axis=-1, keepdims=True))
                    pc = jnp.concatenate([_exp_weights(d_past, m), _exp_weights(d_own, m)], axis=1)
                else:
                    pc = _exp_weights(d_own, m)
                parts.append(pc.astype(BF16))
            p = jnp.concatenate(parts, axis=0)
            acc = jnp.dot(p, va_ref[hh, pl.ds(0, nk), :], preferred_element_type=F32)
            o = acc[:, :HEAD_DIM] / acc[:, HEAD_DIM:]
            z = z_ref[rows, lanes[hh]].astype(F32)
            o_ref[rows, lanes[hh]] = (o * _silu(z)).astype(o_ref.dtype)


def _moba_attention(proj, key_cols):
    b, s, _ = proj.shape
    blk = (None, s, MOBA_HEADS * HEAD_DIM)
    n_steps = N_HEADS // MOBA_HEADS

    def col(c):
        return pl.BlockSpec(blk, lambda hi, bi: (bi, 0, c * n_steps + hi))

    return pl.pallas_call(
        _moba_kernel,
        grid=(n_steps, b),
        in_specs=[col(c) for c in range(4)] + [
            pl.BlockSpec((MOBA_HEADS, s, HEAD_DIM), lambda hi, bi: (hi, 0, 0))],
        out_specs=pl.BlockSpec(blk, lambda hi, bi: (bi, 0, hi)),
        out_shape=jax.ShapeDtypeStruct((b, s, N_HEADS * HEAD_DIM), BF16),
        scratch_shapes=[pltpu.VMEM((MOBA_HEADS, s, 2 * HEAD_DIM), BF16),
                        pltpu.VMEM((MOBA_HEADS, s, 2 * HEAD_DIM), BF16)],
        compiler_params=_params("arbitrary", "arbitrary"),
        name="moba_mixer",
    )(proj, proj, proj, proj, key_cols)


def kernel(x, c, norm_w, mod_w, mod_b, a_w_in, a_w_out, b_w_in, b_w_out, final_norm_w):
    b, s, d = x.shape
    depth = norm_w.shape[0]
    slopes = jnp.exp2(-8.0 * jnp.arange(1, N_HEADS + 1, dtype=F32) / N_HEADS)
    key_cols = _moba_key_columns(N_HEADS, s)
    mod = _modulation(c, mod_w, mod_b)
    norm_w3 = norm_w.reshape(depth, 1, d)
    xf = x.reshape(b * s, d)
    for i in range(depth):
        j = i // 2
        final_w = final_norm_w if i == depth - 1 else None
        if i % 2 == 0:
            proj = _in_proj(_modulate(xf, norm_w3, mod, i, s, DILATIONS), a_w_in, i, j)
            og = _dilated_attention(proj.reshape(b, s, -1), slopes, MOD_TM)
            xf = _out_proj(og.reshape(b * s, -1), a_w_out, xf, mod, i, j, s, final_w)
        else:
            proj = _in_proj(_modulate(xf, norm_w3, mod, i, s, (1,)), b_w_in, i, j)
            og = _moba_attention(proj.reshape(b, s, -1), key_cols)
            xf = _out_proj(og.reshape(b * s, -1), b_w_out, xf, mod, i, j, s, final_w)
    return xf.reshape(b, s, d)
```
